```python
import jax, jax.numpy as jnp
from jax import lax
import numpy as np

D_MODEL = 1024
BATCH = 16
SEQ = 4096
DEPTH = 1
DEC_BATCH = 16
DEC_SEQ = 32
PAST_LEN = 2048

CHUNK = 64
N_META = 16
D_FF = 2816
D_CONV = 1024
CONV_WIDTH = 3
N_HEADS = 16
QK_NOPE = 64
QK_ROPE = 32
QK_DIM = QK_NOPE + QK_ROPE
V_HEAD = 64
Q_LORA = 384
KV_LORA = 128
ROPE_THETA = 10000.0
RMS_EPS = 1e-6
Q_BLOCK = 128
NEG_INF = -1e30
COL_SIZES = (D_CONV, D_CONV, D_CONV, Q_LORA, KV_LORA, QK_ROPE, D_MODEL, D_MODEL)
D_IN_ALL = sum(COL_SIZES)

kernel_name = "hybrid_shortconv_mla_macaron_stream_step"


def rms_norm(x, g):
    xf = x.astype(jnp.float32)
    y = xf * lax.rsqrt(jnp.mean(xf * xf, axis=-1, keepdims=True) + RMS_EPS)
    return (y * g.astype(jnp.float32)).astype(x.dtype)


def half_step_ffn(x, g, w_gate, w_up, w_down):
    h = rms_norm(x, g)
    return x + 0.5 * ((jax.nn.silu(h @ w_gate) * (h @ w_up)) @ w_down)


def rotary(x, pos):
    half = QK_ROPE // 2
    inv_freq = ROPE_THETA ** (-jnp.arange(half, dtype=jnp.float32) / half)
    ang = pos.astype(jnp.float32)[:, None] * inv_freq[None, :]
    cos = jnp.cos(ang)[None, :, None, :]
    sin = jnp.sin(ang)[None, :, None, :]
    xf = x.astype(jnp.float32)
    x1, x2 = xf[..., :half], xf[..., half:]
    return jnp.concatenate([x1 * cos - x2 * sin, x2 * cos + x1 * sin], axis=-1).astype(x.dtype)


def split_columns(z):
    offsets = [int(o) for o in np.cumsum(COL_SIZES)[:-1]]
    return jnp.split(z, offsets, axis=-1)


def project_inputs(h, pos, w_in_all, q_a_norm, w_uq, kv_a_norm, q_norm):
    bsz, L, _ = h.shape
    b_gate, c_gate, v_conv, q_lat, kv_lat, k_pe, g_conv, g_mla = split_columns(h @ w_in_all)
    conv_in = c_gate * v_conv
    q = (rms_norm(q_lat, q_a_norm) @ w_uq).reshape(bsz, L, N_HEADS, QK_DIM)
    q = jnp.concatenate([q[..., :QK_NOPE], rotary(q[..., QK_NOPE:], pos)], axis=-1)
    q = rms_norm(q, q_norm)
    c_kv = rms_norm(kv_lat, kv_a_norm)
    k_pe = rotary(k_pe[:, :, None, :], pos)[:, :, 0, :]
    return b_gate, conv_in, q, c_kv, k_pe, g_conv, g_mla


def expand_keys(c_kv, k_pe, w_ukv, k_norm):
    bsz, L, _ = c_kv.shape
    kv = (c_kv @ w_ukv).reshape(bsz, L, N_HEADS, QK_NOPE + V_HEAD)
    k_rot = jnp.broadcast_to(k_pe[:, :, None, :], (bsz, L, N_HEADS, QK_ROPE))
    k = rms_norm(jnp.concatenate([kv[..., :QK_NOPE], k_rot], axis=-1), k_norm)
    return k, kv[..., QK_NOPE:]


def attend(q, k, v, mask):
    s = jnp.einsum("bqhd,bkhd->bhqk", q, k).astype(jnp.float32) * (QK_DIM ** -0.5)
    if mask is not None:
        s = jnp.where(mask[None, None], s, NEG_INF)
    p = jax.nn.softmax(s, axis=-1).astype(v.dtype)
    return jnp.einsum("bhqk,bkhd->bqhd", p, v)


def prompt_attention(q, k, v):
    bsz, L = q.shape[0], q.shape[1]
    n_blk = -(-L // Q_BLOCK)
    L_pad = n_blk * Q_BLOCK
    chunk_id = (jnp.arange(L_pad, dtype=jnp.int32) - N_META) // CHUNK
    key_chunk = chunk_id[:L]
    q_blocks = jnp.pad(q, ((0, 0), (0, L_pad - L), (0, 0), (0, 0)))
    q_blocks = q_blocks.reshape(bsz, n_blk, Q_BLOCK, N_HEADS, QK_DIM).transpose(1, 0, 2, 3, 4)

    def one_block(args):
        qb, qc = args
        return attend(qb, k, v, key_chunk[None, :] <= qc[:, None])

    o = lax.map(one_block, (q_blocks, chunk_id.reshape(n_blk, Q_BLOCK)))
    return o.transpose(1, 0, 2, 3, 4).reshape(bsz, L_pad, N_HEADS * V_HEAD)[:, :L]


def depthwise_causal_conv(x_ext, w):
    return lax.conv_general_dilated(
        x_ext, w[:, None, :].astype(x_ext.dtype), window_strides=(1,), padding="VALID",
        dimension_numbers=("NWC", "WIO", "NWC"), feature_group_count=x_ext.shape[-1])


def merge_branches(x, b_gate, conv_y, attn, g_conv, g_mla, w_conv_out, w_mla_out, w_out_all):
    conv_branch = (b_gate * conv_y) @ w_conv_out
    mla_branch = attn @ w_mla_out
    merged = jax.nn.sigmoid(g_conv) * conv_branch + jax.nn.sigmoid(g_mla) * mla_branch
    return x + merged @ w_out_all


def setup_inputs(seed: int = 0) -> dict:
    key = jax.random.key(seed)
    ks = jax.random.split(key, 32)
    f32 = jnp.float32

    def nrm(k, shape, scale):
        return jax.random.normal(k, shape, f32) * scale

    def gain(k, dim):
        return 1.0 + 0.02 * jax.random.normal(k, (DEPTH, dim), f32)

    return {
        "x_prompt": nrm(ks[0], (BATCH, SEQ, D_MODEL), 1.0),
        "x_sample": nrm(ks[1], (DEC_BATCH, DEC_SEQ, D_MODEL), 1.0),
        "cache_conv": nrm(ks[2], (DEPTH, DEC_BATCH, CONV_WIDTH - 1, D_CONV), 1.0),
        "cache_kv_latent": nrm(ks[3], (DEPTH, DEC_BATCH, PAST_LEN, KV_LORA), 1.0),
        "cache_k_rope": nrm(ks[4], (DEPTH, DEC_BATCH, PAST_LEN, QK_ROPE), 1.0),
        "meta_tokens": nrm(ks[5], (N_META, D_MODEL), 1.0),
        "ffn1_norm": gain(ks[6], D_MODEL),
        "ffn1_w_gate": nrm(ks[7], (DEPTH, D_MODEL, D_FF), D_MODEL ** -0.5),
        "ffn1_w_up": nrm(ks[8], (DEPTH, D_MODEL, D_FF), D_MODEL ** -0.5),
        "ffn1_w_down": nrm(ks[9], (DEPTH, D_FF, D_MODEL), D_FF ** -0.5),
        "mix_norm": gain(ks[10], D_MODEL),
        "w_in_all": nrm(ks[11], (DEPTH, D_MODEL, D_IN_ALL), D_MODEL ** -0.5),
        "conv_w": nrm(ks[12], (DEPTH, CONV_WIDTH, D_CONV), CONV_WIDTH ** -0.5),
        "w_conv_out": nrm(ks[13], (DEPTH, D_CONV, D_MODEL), D_CONV ** -0.5),
        "q_a_norm": gain(ks[14], Q_LORA),
        "w_uq": nrm(ks[15], (DEPTH, Q_LORA, N_HEADS * QK_DIM), Q_LORA ** -0.5),
        "kv_a_norm": gain(ks[16], KV_LORA),
        "w_ukv": nrm(ks[17], (DEPTH, KV_LORA, N_HEADS * (QK_NOPE + V_HEAD)), KV_LORA ** -0.5),
        "q_norm": gain(ks[18], QK_DIM),
        "k_norm": gain(ks[19], QK_DIM),
        "w_mla_out": nrm(ks[20], (DEPTH, N_HEADS * V_HEAD, D_MODEL), (N_HEADS * V_HEAD) ** -0.5),
        "w_out_all": nrm(ks[21], (DEPTH, D_MODEL, D_MODEL), D_MODEL ** -0.5),
        "ffn2_norm": gain(ks[22], D_MODEL),
        "ffn2_w_gate": nrm(ks[23], (DEPTH, D_MODEL, D_FF), D_MODEL ** -0.5),
        "ffn2_w_up": nrm(ks[24], (DEPTH, D_MODEL, D_FF), D_MODEL ** -0.5),
        "ffn2_w_down": nrm(ks[25], (DEPTH, D_FF, D_MODEL), D_FF ** -0.5),
    }


def reference(x_prompt, x_sample, cache_conv, cache_kv_latent, cache_k_rope, meta_tokens,
              ffn1_norm, ffn1_w_gate, ffn1_w_up, ffn1_w_down, mix_norm, w_in_all, conv_w,
              w_conv_out, q_a_norm, w_uq, kv_a_norm, w_ukv, q_norm, k_norm, w_mla_out,
              w_out_all, ffn2_norm, ffn2_w_gate, ffn2_w_up, ffn2_w_down):
    bsz_p = x_prompt.shape[0]
    meta = jnp.broadcast_to(meta_tokens[None].astype(x_prompt.dtype), (bsz_p, N_META, D_MODEL))
    xp = jnp.concatenate([meta, x_prompt], axis=1)
    xs = x_sample
    L_p, L_s = xp.shape[1], xs.shape[1]
    pos_p = jnp.arange(L_p, dtype=jnp.int32)
    pos_s = N_META + PAST_LEN + jnp.arange(L_s, dtype=jnp.int32)

    conv_p_rows, ckv_p_rows, kpe_p_rows = [], [], []
    conv_s_rows, ckv_s_rows, kpe_s_rows = [], [], []
    for l in range(DEPTH):
        xp = half_step_ffn(xp, ffn1_norm[l], ffn1_w_gate[l], ffn1_w_up[l], ffn1_w_down[l])
        xs = half_step_ffn(xs, ffn1_norm[l], ffn1_w_gate[l], ffn1_w_up[l], ffn1_w_down[l])

        bp, cin_p, qp, ckv_p, kpe_p, gcp, gmp = project_inputs(
            rms_norm(xp, mix_norm[l]), pos_p, w_in_all[l], q_a_norm[l], w_uq[l], kv_a_norm[l], q_norm[l])
        bs, cin_s, qs, ckv_s, kpe_s, gcs, gms = project_inputs(
            rms_norm(xs, mix_norm[l]), pos_s, w_in_all[l], q_a_norm[l], w_uq[l], kv_a_norm[l], q_norm[l])

        ext_p = jnp.pad(cin_p, ((0, 0), (CONV_WIDTH - 1, 0), (0, 0)))
        conv_y_p = depthwise_causal_conv(ext_p, conv_w[l])
        ext_s = jnp.concatenate([cache_conv[l].astype(cin_s.dtype), cin_s], axis=1)
        conv_y_s = depthwise_causal_conv(ext_s, conv_w[l])

        k_p, v_p = expand_keys(ckv_p, kpe_p, w_ukv[l], k_norm[l])
        attn_p = prompt_attention(qp, k_p, v_p)
        ckv_all = jnp.concatenate([cache_kv_latent[l].astype(ckv_s.dtype), ckv_s], axis=1)
        kpe_all = jnp.concatenate([cache_k_rope[l].astype(kpe_s.dtype), kpe_s], axis=1)
        k_s, v_s = expand_keys(ckv_all, kpe_all, w_ukv[l], k_norm[l])
        attn_s = attend(qs, k_s, v_s, None).reshape(xs.shape[0], L_s, N_HEADS * V_HEAD)

        xp = merge_branches(xp, bp, conv_y_p, attn_p, gcp, gmp, w_conv_out[l], w_mla_out[l], w_out_all[l])
        xs = merge_branches(xs, bs, conv_y_s, attn_s, gcs, gms, w_conv_out[l], w_mla_out[l], w_out_all[l])

        xp = half_step_ffn(xp, ffn2_norm[l], ffn2_w_gate[l], ffn2_w_up[l], ffn2_w_down[l])
        xs = half_step_ffn(xs, ffn2_norm[l], ffn2_w_gate[l], ffn2_w_up[l], ffn2_w_down[l])

        conv_p_rows.append(ext_p[:, -(CONV_WIDTH - 1):])
        ckv_p_rows.append(ckv_p)
        kpe_p_rows.append(kpe_p)
        conv_s_rows.append(ext_s[:, -(CONV_WIDTH - 1):])
        ckv_s_rows.append(ckv_s)
        kpe_s_rows.append(kpe_s)

    y_prompt = xp[:, N_META:]
    y_sample = xs
    new_conv_prompt = jnp.stack(conv_p_rows)
    new_kv_latent_prompt = jnp.stack(ckv_p_rows)
    new_k_rope_prompt = jnp.stack(kpe_p_rows)
    new_conv_sample = jnp.stack(conv_s_rows)
    new_kv_latent_sample = jnp.stack(ckv_s_rows)
    new_k_rope_sample = jnp.stack(kpe_s_rows)
    return (y_prompt, y_sample, new_conv_prompt, new_kv_latent_prompt, new_k_rope_prompt,
            new_conv_sample, new_kv_latent_sample, new_k_rope_sample)
```

```python
import functools

import jax
import jax.numpy as jnp
from jax import lax
from jax.experimental import pallas as pl
from jax.experimental.pallas import tpu as pltpu

D_MODEL = 1024
D_FF = 2816
D_CONV = 1024
CONV_WIDTH = 3
N_HEADS = 16
QK_NOPE = 64
QK_ROPE = 32
QK_DIM = QK_NOPE + QK_ROPE
V_HEAD = 64
Q_LORA = 384
KV_LORA = 128
N_META = 16
CHUNK = 64
ROPE_THETA = 10000.0
RMS_EPS = 1e-6
NEG_INF = -1e30

HEAD_PAD = 128
QK_W = N_HEADS * HEAD_PAD
V_W = N_HEADS * V_HEAD
OFF_B, OFF_C, OFF_V, OFF_GC, OFF_GM = 0, 1024, 2048, 3072, 4096
OFF_QL = 5120
OFF_KV = OFF_QL + Q_LORA
OFF_KPE = OFF_KV + KV_LORA
D_IN_P = OFF_KPE + HEAD_PAD

VMEM_LIMIT = 56 * 1024 * 1024

F32 = jnp.float32
BF16 = jnp.bfloat16


def _rms(x, g):
    return x * lax.rsqrt(jnp.mean(x * x, axis=-1, keepdims=True) + RMS_EPS) * g


def _dot(a, b):
    return jnp.dot(a, b, preferred_element_type=F32)


def _dot_t(a, b):
    return lax.dot_general(a, b, (((1,), (1,)), ((), ())), preferred_element_type=F32)


def _full(shape):
    return pl.BlockSpec(shape, lambda *_: (0,) * len(shape))


def _ffn_kernel(x_ref, g_ref, wg_ref, wu_ref, wd_ref, o_ref):
    x = x_ref[...]
    h = _rms(x, g_ref[...]).astype(BF16)
    gate = _dot(h, wg_ref[...])
    up = _dot(h, wu_ref[...])
    a = (gate * jax.nn.sigmoid(gate) * up).astype(BF16)
    o_ref[...] = x + 0.5 * _dot(a, wd_ref[...])


def _ffn(x, g, wg, wu, wd, tm):
    n = x.shape[0]
    assert n % tm == 0
    return pl.pallas_call(
        _ffn_kernel,
        grid=(n // tm,),
        in_specs=[
            pl.BlockSpec((tm, D_MODEL), lambda i: (i, 0)),
            _full((1, D_MODEL)),
            _full((D_MODEL, D_FF)),
            _full((D_MODEL, D_FF)),
            _full((D_FF, D_MODEL)),
        ],
        out_specs=pl.BlockSpec((tm, D_MODEL), lambda i: (i, 0)),
        out_shape=jax.ShapeDtypeStruct((n, D_MODEL), F32),
        compiler_params=pltpu.CompilerParams(
            dimension_semantics=("arbitrary",), vmem_limit_bytes=VMEM_LIMIT),
        name="ffn",
    )(x, g, wg, wu, wd)


def _rope(blk, rc, rs1, rs2):
    return blk * rc + pltpu.roll(blk, HEAD_PAD - QK_ROPE // 2, 1) * rs1 + pltpu.roll(blk, QK_ROPE // 2, 1) * rs2


def _expand_keys(cb, kpr, wuk_ref, kgain, store):
    sspe = jnp.sum(kpr * kpr, axis=-1, keepdims=True)
    for hd in range(N_HEADS):
        sl = slice(hd * HEAD_PAD, (hd + 1) * HEAD_PAD)
        kn = _dot(cb, wuk_ref[:, sl])
        ss = (jnp.sum(kn * kn, axis=-1, keepdims=True) + sspe) * (1.0 / QK_DIM)
        store(hd, ((kn + kpr) * lax.rsqrt(ss + RMS_EPS) * kgain).astype(BF16))


def _proj_kernel(x_ref, g_ref, win_ref, cw_ref, cinit_ref, qag_ref, wuq_ref, kvag_ref, wuk_ref,
                 wuv_ref, qgain_ref, kgain_ref, rc_ref, rs1_ref, rs2_ref,
                 bc_ref, sgc_ref, sgm_ref, q_ref, k_ref, v_ref, ckv_ref, kpe_ref, tail_ref, cbuf):
    tm = x_ref.shape[0]

    @pl.when(pl.program_id(1) == 0)
    def _():
        cbuf[6:8, :] = cinit_ref[0]

    h = _rms(x_ref[...], g_ref[...]).astype(BF16)

    def proj(lo, n):
        return _dot(h, win_ref[:, lo:lo + n])

    cin = proj(OFF_C, D_CONV) * proj(OFF_V, D_CONV)
    cbuf[8:8 + tm, :] = cin
    cw = cw_ref[...]
    y = cw[0:1] * cbuf[6:6 + tm, :] + cw[1:2] * cbuf[7:7 + tm, :] + cw[2:3] * cin
    bc_ref[...] = (proj(OFF_B, D_CONV) * y).astype(BF16)
    tail = cbuf[tm + 6:tm + 8, :]
    cbuf[6:8, :] = tail
    tail_ref[0] = tail

    sgc_ref[...] = jax.nn.sigmoid(proj(OFF_GC, D_MODEL))
    sgm_ref[...] = jax.nn.sigmoid(proj(OFF_GM, D_MODEL))

    rc, rs1, rs2 = rc_ref[...], rs1_ref[...], rs2_ref[...]

    qn = _rms(proj(OFF_QL, Q_LORA), qag_ref[...]).astype(BF16)
    qgain = qgain_ref[...]
    for hd in range(N_HEADS):
        sl = slice(hd * HEAD_PAD, (hd + 1) * HEAD_PAD)
        rot = _rope(_dot(qn, wuq_ref[:, sl]), rc, rs1, rs2)
        ss = jnp.sum(rot * rot, axis=-1, keepdims=True) * (1.0 / QK_DIM)
        q_ref[:, sl] = (rot * lax.rsqrt(ss + RMS_EPS) * qgain).astype(BF16)

    ckv = _rms(proj(OFF_KV, KV_LORA), kvag_ref[...])
    ckv_ref[...] = ckv
    cb = ckv.astype(BF16)
    v_ref[...] = _dot(cb, wuv_ref[...]).astype(BF16)
    kpr = _rope(proj(OFF_KPE, HEAD_PAD), rc, rs1, rs2)
    kpe_ref[...] = kpr[:, QK_NOPE:QK_DIM]

    def store_k(hd, val):
        k_ref[:, hd * HEAD_PAD:(hd + 1) * HEAD_PAD] = val

    _expand_keys(cb, kpr, wuk_ref, kgain_ref[...], store_k)


def _proj(x, cinit, rope_tabs, rope_per_batch, w, nb, rows, tm):
    nt = rows // tm
    assert rows % tm == 0
    n = nb * rows
    tok = lambda width: pl.BlockSpec((tm, width), lambda b, j: (b * nt + j, 0))
    if rope_per_batch:
        rope = pl.BlockSpec((tm, HEAD_PAD), lambda b, j: (b * nt + j, 0))
    else:
        rope = pl.BlockSpec((tm, HEAD_PAD), lambda b, j: (j, 0))
    out_shapes = (
        jax.ShapeDtypeStruct((n, D_CONV), BF16),
        jax.ShapeDtypeStruct((n, D_MODEL), F32),
        jax.ShapeDtypeStruct((n, D_MODEL), F32),
        jax.ShapeDtypeStruct((n, QK_W), BF16),
        jax.ShapeDtypeStruct((n, QK_W), BF16),
        jax.ShapeDtypeStruct((n, V_W), BF16),
        jax.ShapeDtypeStruct((n, KV_LORA), F32),
        jax.ShapeDtypeStruct((n, QK_ROPE), F32),
        jax.ShapeDtypeStruct((nb, 2, D_CONV), F32),
    )
    return pl.pallas_call(
        _proj_kernel,
        grid=(nb, nt),
        in_specs=[
            tok(D_MODEL), _full((1, D_MODEL)), _full((D_MODEL, D_IN_P)), _full((CONV_WIDTH, D_CONV)),
            pl.BlockSpec((1, 2, D_CONV), lambda b, j: (b, 0, 0)),
            _full((1, Q_LORA)), _full((Q_LORA, QK_W)), _full((1, KV_LORA)), _full((KV_LORA, QK_W)),
            _full((KV_LORA, V_W)), _full((1, HEAD_PAD)), _full((1, HEAD_PAD)), rope, rope, rope,
        ],
        out_specs=(tok(D_CONV), tok(D_MODEL), tok(D_MODEL), tok(QK_W), tok(QK_W), tok(V_W),
                   tok(KV_LORA), tok(QK_ROPE), pl.BlockSpec((1, 2, D_CONV), lambda b, j: (b, 0, 0))),
        out_shape=out_shapes,
        scratch_shapes=[pltpu.VMEM((tm + 8, D_CONV), F32)],
        compiler_params=pltpu.CompilerParams(
            dimension_semantics=("arbitrary", "arbitrary"), vmem_limit_bytes=VMEM_LIMIT),
        name="proj",
    )(x, w["mix_norm"], w["w_in"], w["conv_w"], cinit, w["q_a_norm"], w["w_uq"], w["kv_a_norm"],
      w["w_uk"], w["w_uv"], w["qgain"], w["kgain"], *rope_tabs)


def _attn_prompt_kernel(q_ref, k_ref, v_ref, km_ref, vm_ref, o_ref, *, tq):
    seq = q_ref.shape[1]
    nq = seq // tq
    is_a = lax.broadcasted_iota(jnp.int32, (tq, 2 * V_HEAD), 1) < V_HEAD
    rchunk = lax.broadcasted_iota(jnp.int32, (tq, tq), 0) // CHUNK
    cchunk = lax.broadcasted_iota(jnp.int32, (tq, tq), 1) // CHUNK
    diag_mask = cchunk <= rchunk
    vm = vm_ref[...]

    def qblock(qi, carry):
        q0 = pl.multiple_of(qi * tq, tq)
        qs = [q_ref[0, pl.ds(q0, tq), hh * HEAD_PAD:(hh + 1) * HEAD_PAD] for hh in range(2)]

        state = []
        for hh in range(2):
            s = _dot_t(qs[hh], km_ref[:, hh * HEAD_PAD:(hh + 1) * HEAD_PAD])
            m = jnp.max(s, axis=-1, keepdims=True)
            p = jnp.exp(s - m)
            state.append((m, jnp.sum(p, axis=-1, keepdims=True), _dot(p.astype(BF16), vm)))

        def step(k0, state, masked):
            vblk = v_ref[0, pl.ds(k0, tq), :]
            out = []
            for hh in range(2):
                m, l, acc = state[hh]
                s = _dot_t(qs[hh], k_ref[0, pl.ds(k0, tq), hh * HEAD_PAD:(hh + 1) * HEAD_PAD])
                if masked:
                    s = jnp.where(diag_mask, s, NEG_INF)
                m_new = jnp.maximum(m, jnp.max(s, axis=-1, keepdims=True))
                alpha = jnp.exp(m - m_new)
                p = jnp.exp(s - m_new)
                out.append((m_new, alpha * l + jnp.sum(p, axis=-1, keepdims=True),
                            alpha * acc + _dot(p.astype(BF16), vblk)))
            return tuple(out)

        state = lax.fori_loop(0, qi, lambda kj, st: step(pl.multiple_of(kj * tq, tq), st, False),
                              tuple(state))
        (_, l_a, acc_a), (_, l_b, acc_b) = step(q0, state, True)
        o_ref[0, pl.ds(q0, tq), :] = jnp.where(is_a, acc_a / l_a, acc_b / l_b).astype(BF16)
        return carry

    lax.fori_loop(0, nq, qblock, 0)


def _attn_prompt(q, k, v, k_meta, v_meta, tq=256):
    nb, seq, _ = q.shape
    pair = lambda width: pl.BlockSpec((1, seq, width), lambda b, p: (b, 0, p))
    return pl.pallas_call(
        functools.partial(_attn_prompt_kernel, tq=tq),
        grid=(nb, N_HEADS // 2),
        in_specs=[pair(2 * HEAD_PAD), pair(2 * HEAD_PAD), pair(2 * V_HEAD),
                  pl.BlockSpec((N_META, 2 * HEAD_PAD), lambda b, p: (0, p)),
                  pl.BlockSpec((N_META, 2 * V_HEAD), lambda b, p: (0, p))],
        out_specs=pair(2 * V_HEAD),
        out_shape=jax.ShapeDtypeStruct((nb, seq, V_W), BF16),
        compiler_params=pltpu.CompilerParams(
            dimension_semantics=("arbitrary", "arbitrary"), vmem_limit_bytes=VMEM_LIMIT),
        name="attn_prompt",
    )(q, k, v, k_meta, v_meta)


def _attn_sample_kernel(q_ref, k_ref, v_ref, ckv_ref, kpe_ref, wuk_ref, wuv_ref, kgain_ref, o_ref, kc_ref):
    rows = q_ref.shape[0]
    cb = ckv_ref[0].astype(BF16)
    is_a = lax.broadcasted_iota(jnp.int32, (rows, 2 * V_HEAD), 1) < V_HEAD

    def store_k(hd, val):
        kc_ref[hd] = val

    _expand_keys(cb, kpe_ref[0], wuk_ref, kgain_ref[...], store_k)

    for pr in range(N_HEADS // 2):
        vsl = slice(pr * 2 * V_HEAD, (pr + 1) * 2 * V_HEAD)
        v_cache = _dot(cb, wuv_ref[:, vsl]).astype(BF16)
        v_new = v_ref[:, vsl]
        outs = []
        for hh in range(2):
            hd = 2 * pr + hh
            sl = slice(hd * HEAD_PAD, (hd + 1) * HEAD_PAD)
            qh = q_ref[:, sl]
            s1 = _dot_t(qh, kc_ref[hd])
            s2 = _dot_t(qh, k_ref[:, sl])
            m = jnp.maximum(jnp.max(s1, axis=-1, keepdims=True), jnp.max(s2, axis=-1, keepdims=True))
            p1 = jnp.exp(s1 - m)
            p2 = jnp.exp(s2 - m)
            l = jnp.sum(p1, axis=-1, keepdims=True) + jnp.sum(p2, axis=-1, keepdims=True)
            outs.append((_dot(p1.astype(BF16), v_cache) + _dot(p2.astype(BF16), v_new)) / l)
        o_ref[:, vsl] = jnp.where(is_a, outs[0], outs[1]).astype(BF16)


def _attn_sample(q, k, v, ckv_cache, kpe_cache_p, w, nb, rows):
    past = ckv_cache.shape[1]
    tok = lambda width: pl.BlockSpec((rows, width), lambda b: (b, 0))
    return pl.pallas_call(
        _attn_sample_kernel,
        grid=(nb,),
        in_specs=[tok(QK_W), tok(QK_W), tok(V_W),
                  pl.BlockSpec((1, past, KV_LORA), lambda b: (b, 0, 0)),
                  pl.BlockSpec((1, past, HEAD_PAD), lambda b: (b, 0, 0)),
                  _full((KV_LORA, QK_W)), _full((KV_LORA, V_W)), _full((1, HEAD_PAD))],
        out_specs=tok(V_W),
        out_shape=jax.ShapeDtypeStruct((nb * rows, V_W), BF16),
        scratch_shapes=[pltpu.VMEM((N_HEADS, past, HEAD_PAD), BF16)],
        compiler_params=pltpu.CompilerParams(
            dimension_semantics=("arbitrary",), vmem_limit_bytes=VMEM_LIMIT),
        name="attn_sample",
    )(q, k, v, ckv_cache, kpe_cache_p, w["w_uk"], w["w_uv"], w["kgain"])


def _merge_kernel(x_ref, bc_ref, at_ref, sgc_ref, sgm_ref, wco_ref, wmo_ref, woa_ref, o_ref):
    merged = sgc_ref[...] * _dot(bc_ref[...], wco_ref[...]) + sgm_ref[...] * _dot(at_ref[...], wmo_ref[...])
    o_ref[...] = x_ref[...] + _dot(merged.astype(BF16), woa_ref[...])


def _merge(x, bc, attn, sgc, sgm, w, tm):
    n = x.shape[0]
    assert n % tm == 0
    tok = pl.BlockSpec((tm, D_MODEL), lambda i: (i, 0))
    sq = _full((D_MODEL, D_MODEL))
    return pl.pallas_call(
        _merge_kernel,
        grid=(n // tm,),
        in_specs=[tok, tok, tok, tok, tok, sq, sq, sq],
        out_specs=tok,
        out_shape=jax.ShapeDtypeStruct((n, D_MODEL), F32),
        compiler_params=pltpu.CompilerParams(
            dimension_semantics=("arbitrary",), vmem_limit_bytes=VMEM_LIMIT),
        name="merge",
    )(x, bc, attn, sgc, sgm, w["w_conv_out"], w["w_mla_out"], w["w_out_all"])


def _rope_tables(pos):
    half = QK_ROPE // 2
    inv_freq = ROPE_THETA ** (-jnp.arange(half, dtype=F32) / half)
    ang = pos.astype(F32)[:, None] * inv_freq[None, :]
    cos, sin = jnp.cos(ang), jnp.sin(ang)
    n = pos.shape[0]
    one = jnp.ones((n, QK_NOPE), F32)
    z16 = jnp.zeros((n, half), F32)
    z32 = jnp.zeros((n, HEAD_PAD - QK_DIM), F32)
    z64 = jnp.zeros((n, QK_NOPE), F32)
    rc = jnp.concatenate([one, cos, cos, z32], axis=1)
    rs1 = jnp.concatenate([z64, -sin, z16, z32], axis=1)
    rs2 = jnp.concatenate([z64, z16, sin, z32], axis=1)
    return rc, rs1, rs2


def _prep_weights(mix_norm, w_in_all, conv_w, w_conv_out, q_a_norm, w_uq, kv_a_norm, w_ukv, q_norm,
                  k_norm, w_mla_out, w_out_all):
    o_ql = 3 * D_CONV
    o_kv = o_ql + Q_LORA
    o_kpe = o_kv + KV_LORA
    o_gc = o_kpe + QK_ROPE
    zeros = lambda n: jnp.zeros((D_MODEL, n), w_in_all.dtype)
    w_in = jnp.concatenate(
        [w_in_all[:, :o_ql], w_in_all[:, o_gc:], w_in_all[:, o_ql:o_kpe], zeros(QK_NOPE),
         w_in_all[:, o_kpe:o_gc], zeros(HEAD_PAD - QK_DIM)], axis=1).astype(BF16)
    pad_head = lambda a: jnp.pad(a, ((0, 0), (0, 0), (0, HEAD_PAD - a.shape[-1]))).reshape(a.shape[0], QK_W)
    w_ukv3 = w_ukv.reshape(KV_LORA, N_HEADS, QK_NOPE + V_HEAD)
    pad_gain = lambda g: jnp.pad(g, (0, HEAD_PAD - QK_DIM)).reshape(1, HEAD_PAD).astype(F32)
    return {
        "mix_norm": mix_norm.reshape(1, D_MODEL),
        "w_in": w_in,
        "conv_w": conv_w,
        "q_a_norm": q_a_norm.reshape(1, Q_LORA),
        "w_uq": pad_head(w_uq.reshape(Q_LORA, N_HEADS, QK_DIM)).astype(BF16),
        "kv_a_norm": kv_a_norm.reshape(1, KV_LORA),
        "w_uk": pad_head(w_ukv3[..., :QK_NOPE]).astype(BF16),
        "w_uv": w_ukv3[..., QK_NOPE:].reshape(KV_LORA, V_W).astype(BF16),
        "qgain": pad_gain(q_norm * (QK_DIM ** -0.5)),
        "kgain": pad_gain(k_norm),
        "w_conv_out": w_conv_out.astype(BF16),
        "w_mla_out": w_mla_out.astype(BF16),
        "w_out_all": w_out_all.astype(BF16),
    }


def kernel(x_prompt, x_sample, cache_conv, cache_kv_latent, cache_k_rope, meta_tokens, ffn1_norm, ffn1_w_gate, ffn1_w_up, ffn1_w_down, mix_norm, w_in_all, conv_w, w_conv_out, q_a_norm, w_uq, kv_a_norm, w_ukv, q_norm, k_norm, w_mla_out, w_out_all, ffn2_norm, ffn2_w_gate, ffn2_w_up, ffn2_w_down):
    depth = ffn1_norm.shape[0]
    assert depth == 1
    nb, seq, _ = x_prompt.shape
    db, dseq, _ = x_sample.shape
    past = cache_kv_latent.shape[2]
    assert dseq == 2 * N_META and seq % 256 == 0

    w = _prep_weights(mix_norm[0], w_in_all[0], conv_w[0], w_conv_out[0], q_a_norm[0], w_uq[0],
                      kv_a_norm[0], w_ukv[0], q_norm[0], k_norm[0], w_mla_out[0], w_out_all[0])
    ffn1 = (ffn1_norm[0].reshape(1, D_MODEL), ffn1_w_gate[0].astype(BF16), ffn1_w_up[0].astype(BF16),
            ffn1_w_down[0].astype(BF16))
    ffn2 = (ffn2_norm[0].reshape(1, D_MODEL), ffn2_w_gate[0].astype(BF16), ffn2_w_up[0].astype(BF16),
            ffn2_w_down[0].astype(BF16))

    n_s = db * dseq
    xs = jnp.concatenate([x_sample.reshape(n_s, D_MODEL), jnp.zeros((dseq - N_META, D_MODEL), F32),
                          meta_tokens.astype(F32)], axis=0)
    xf = x_prompt.reshape(nb * seq, D_MODEL)

    x1s = _ffn(xs, *ffn1, tm=xs.shape[0])
    x1f = _ffn(xf, *ffn1, tm=512)

    pos_s = N_META + past + jnp.arange(dseq, dtype=jnp.int32)
    pos_m = jnp.concatenate([jnp.zeros((dseq - N_META,), jnp.int32), jnp.arange(N_META, dtype=jnp.int32)])
    rope_s = _rope_tables(jnp.concatenate([jnp.tile(pos_s, db), pos_m]))
    rope_f = _rope_tables(N_META + jnp.arange(seq, dtype=jnp.int32))

    cinit_s = jnp.concatenate([cache_conv[0].astype(F32), jnp.zeros((1, 2, D_CONV), F32)], axis=0)
    (bc_s, sgc_s, sgm_s, q_s, k_s, v_s, ckv_s, kpe_s, tail_s) = _proj(
        x1s, cinit_s, rope_s, True, w, db + 1, dseq, dseq)
    cinit_f = jnp.broadcast_to(tail_s[db:db + 1], (nb, 2, D_CONV))
    (bc_f, sgc_f, sgm_f, q_f, k_f, v_f, ckv_f, kpe_f, tail_f) = _proj(
        x1f, cinit_f, rope_f, False, w, nb, seq, 256)

    m0 = n_s + dseq - N_META
    attn_f = _attn_prompt(q_f.reshape(nb, seq, QK_W), k_f.reshape(nb, seq, QK_W),
                          v_f.reshape(nb, seq, V_W), k_s[m0:], v_s[m0:])
    kpe_cache_p = jnp.pad(cache_k_rope[0].astype(F32), ((0, 0), (0, 0), (QK_NOPE, HEAD_PAD - QK_DIM)))
    attn_s = _attn_sample(q_s[:n_s], k_s[:n_s], v_s[:n_s], cache_kv_latent[0].astype(F32), kpe_cache_p,
                          w, db, dseq)

    x2f = _merge(x1f, bc_f, attn_f.reshape(nb * seq, V_W), sgc_f, sgm_f, w, 512)
    x2s = _merge(x1s[:n_s], bc_s[:n_s], attn_s, sgc_s[:n_s], sgm_s[:n_s], w, n_s)

    y_prompt = _ffn(x2f, *ffn2, tm=512).reshape(nb, seq, D_MODEL)
    y_sample = _ffn(x2s, *ffn2, tm=n_s).reshape(db, dseq, D_MODEL)

    meta_rows = lambda a: jnp.broadcast_to(a[m0:][None], (nb, N_META, a.shape[-1]))
    new_kv_p = jnp.concatenate([meta_rows(ckv_s), ckv_f.reshape(nb, seq, KV_LORA)], axis=1)
    new_kpe_p = jnp.concatenate([meta_rows(kpe_s), kpe_f.reshape(nb, seq, QK_ROPE)], axis=1)
    return (y_prompt, y_sample, tail_f[None], new_kv_p[None], new_kpe_p[None], tail_s[:db][None],
            ckv_s[:n_s].reshape(db, dseq, KV_LORA)[None], kpe_s[:n_s].reshape(db, dseq, QK_ROPE)[None])
```

```python
import functools

import jax
import jax.numpy as jnp
from jax import lax
from jax.experimental import pallas as pl
from jax.experimental.pallas import tpu as pltpu

D_MODEL = 1024
D_FF = 2816
D_CONV = 1024
CONV_WIDTH = 3
N_HEADS = 16
QK_NOPE = 64
QK_ROPE = 32
QK_DIM = QK_NOPE + QK_ROPE
V_HEAD = 64
Q_LORA = 384
KV_LORA = 128
N_META = 16
CHUNK = 64
ROPE_THETA = 10000.0
RMS_EPS = 1e-6
NEG_INF = -1e30

HEAD_PAD = 128
QK_W = N_HEADS * HEAD_PAD
V_W = N_HEADS * V_HEAD
SMALL_ROWS = 128
SUM_ROWS = 16
LOG2E = 1.4426950408889634
OFF_B, OFF_C, OFF_V, OFF_GC, OFF_GM = 0, 1024, 2048, 3072, 4096
OFF_QL = 5120
OFF_KV = OFF_QL + Q_LORA
OFF_KPE = OFF_KV + KV_LORA
D_IN_P = OFF_KPE + HEAD_PAD

VMEM_LIMIT = 58 * 1024 * 1024

F32 = jnp.float32
BF16 = jnp.bfloat16


def _rms(x, g):
    return x * lax.rsqrt(jnp.mean(x * x, axis=-1, keepdims=True) + RMS_EPS) * g


def _dot(a, b):
    return jnp.dot(a, b, preferred_element_type=F32)


def _dot_t(a, b):
    return lax.dot_general(a, b, (((1,), (1,)), ((), ())), preferred_element_type=F32)


def _full(shape):
    return pl.BlockSpec(shape, lambda *_: (0,) * len(shape))


def _head(hd):
    return slice(hd * HEAD_PAD, (hd + 1) * HEAD_PAD)


def _ffn_kernel(x_ref, g_ref, wg_ref, wu_ref, wd_ref, o_ref):
    x = x_ref[...]
    h = _rms(x, g_ref[...]).astype(BF16)
    gate = _dot(h, wg_ref[...])
    up = _dot(h, wu_ref[...])
    a = (gate * jax.nn.sigmoid(gate) * up).astype(BF16)
    o_ref[...] = x + 0.5 * _dot(a, wd_ref[...])


def _ffn(x, g, wg, wu, wd, tm):
    n = x.shape[0]
    assert n % tm == 0
    return pl.pallas_call(
        _ffn_kernel,
        grid=(n // tm,),
        in_specs=[
            pl.BlockSpec((tm, D_MODEL), lambda i: (i, 0)),
            _full((1, D_MODEL)),
            _full((D_MODEL, D_FF)),
            _full((D_MODEL, D_FF)),
            _full((D_FF, D_MODEL)),
        ],
        out_specs=pl.BlockSpec((tm, D_MODEL), lambda i: (i, 0)),
        out_shape=jax.ShapeDtypeStruct((n, D_MODEL), F32),
        compiler_params=pltpu.CompilerParams(
            dimension_semantics=("arbitrary",), vmem_limit_bytes=VMEM_LIMIT),
        name="ffn",
    )(x, g, wg, wu, wd)


def _expand_keys(cb, kpr, wuk_ref, kgain, store):
    sspe = jnp.sum(kpr * kpr, axis=-1, keepdims=True)
    for hd in range(N_HEADS):
        kn = _dot(cb, wuk_ref[:, _head(hd)])
        ss = (jnp.sum(kn * kn, axis=-1, keepdims=True) + sspe) * (1.0 / QK_DIM)
        store(hd, ((kn + kpr) * lax.rsqrt(ss + RMS_EPS) * kgain).astype(BF16))


def _proj_kernel(x_ref, g_ref, win_ref, cw_ref, cinit_ref, qag_ref, wuqt_ref, kvag_ref, wukt_ref,
                 wuvt_ref, qgain_ref, kgain_ref, rc_ref, rs1_ref, rs2_ref, cost_ref, sint_ref,
                 bc_ref, sgc_ref, sgm_ref, qt_ref, k_ref, vt_ref, ckv_ref, kpe_ref, tail_ref, cbuf,
                 *, tail_row):
    tm = x_ref.shape[0]
    half = QK_ROPE // 2

    @pl.when(pl.program_id(1) == 0)
    def _():
        cbuf[6:8, :] = cinit_ref[0]

    h = _rms(x_ref[...], g_ref[...]).astype(BF16)

    def proj(lo, n):
        return _dot(h, win_ref[:, lo:lo + n])

    cin = proj(OFF_C, D_CONV) * proj(OFF_V, D_CONV)
    cbuf[8:8 + tm, :] = cin
    cw = cw_ref[...]
    y = cw[0:1] * cbuf[6:6 + tm, :] + cw[1:2] * cbuf[7:7 + tm, :] + cw[2:3] * cin
    bc_ref[...] = (proj(OFF_B, D_CONV) * y).astype(BF16)
    tail = cbuf[tail_row + 6:tail_row + 8, :]
    cbuf[6:8, :] = tail
    tail_ref[0] = tail

    sgc_ref[...] = jax.nn.sigmoid(proj(OFF_GC, D_MODEL)).astype(BF16)
    sgm_ref[...] = jax.nn.sigmoid(proj(OFF_GM, D_MODEL)).astype(BF16)

    qn = _rms(proj(OFF_QL, Q_LORA), qag_ref[...]).astype(BF16)
    qt = _dot_t(wuqt_ref[...], qn)
    cos_t, sin_t = cost_ref[...], sint_ref[...]
    qgain = jnp.broadcast_to(qgain_ref[...], (HEAD_PAD, tm))
    for hd in range(N_HEADS):
        r0 = hd * HEAD_PAD
        x1 = qt[r0 + QK_NOPE:r0 + QK_NOPE + half]
        x2 = qt[r0 + QK_NOPE + half:r0 + QK_DIM]
        rot = jnp.concatenate([qt[r0:r0 + QK_NOPE], x1 * cos_t - x2 * sin_t, x2 * cos_t + x1 * sin_t,
                               qt[r0 + QK_DIM:r0 + HEAD_PAD]], axis=0)
        ss = jnp.sum(rot * rot, axis=0, keepdims=True) * (1.0 / QK_DIM)
        qt_ref[0, _head(hd), :] = (rot * lax.rsqrt(ss + RMS_EPS) * qgain).astype(BF16)

    ckv = _rms(proj(OFF_KV, KV_LORA), kvag_ref[...])
    ckv_ref[...] = ckv
    cb = ckv.astype(BF16)
    kpb = proj(OFF_KPE, HEAD_PAD)
    kpr = (kpb * rc_ref[...] + pltpu.roll(kpb, HEAD_PAD - half, 1) * rs1_ref[...]
           + pltpu.roll(kpb, half, 1) * rs2_ref[...])
    kpe_ref[...] = kpr[:, QK_NOPE:QK_DIM]

    vt_ref[0] = _dot_t(wuvt_ref[...], cb).astype(BF16)
    knt = _dot_t(wukt_ref[...], cb)
    kprt = kpr.T
    sspe = jnp.sum(kprt * kprt, axis=0, keepdims=True)
    kgain = jnp.broadcast_to(kgain_ref[...], (HEAD_PAD, tm))
    for hd in range(N_HEADS):
        kn = knt[_head(hd)]
        ss = (jnp.sum(kn * kn, axis=0, keepdims=True) + sspe) * (1.0 / QK_DIM)
        kt = (kn + kprt) * lax.rsqrt(ss + RMS_EPS) * kgain
        k_ref[:, _head(hd)] = kt.T.astype(BF16)


def _proj(x, cinit, rope_tok, rope_feat, rope_per_stream, w, nb, rows, tm, tail_row):
    nt = rows // tm
    assert rows % tm == 0 and tm % 128 == 0
    n = nb * rows
    tok = lambda width: pl.BlockSpec((tm, width), lambda b, j: (b * nt + j, 0))
    feat = lambda width: pl.BlockSpec((1, width, tm), lambda b, j: (b, 0, j))
    if rope_per_stream:
        rope_t = pl.BlockSpec((tm, HEAD_PAD), lambda b, j: (b * nt + j, 0))
        rope_f = pl.BlockSpec((QK_ROPE // 2, tm), lambda b, j: (0, b * nt + j))
    else:
        rope_t = pl.BlockSpec((tm, HEAD_PAD), lambda b, j: (j, 0))
        rope_f = pl.BlockSpec((QK_ROPE // 2, tm), lambda b, j: (0, j))
    per_stream = pl.BlockSpec((1, 2, D_CONV), lambda b, j: (b, 0, 0))
    out_shapes = (
        jax.ShapeDtypeStruct((n, D_CONV), BF16),
        jax.ShapeDtypeStruct((n, D_MODEL), BF16),
        jax.ShapeDtypeStruct((n, D_MODEL), BF16),
        jax.ShapeDtypeStruct((nb, QK_W, rows), BF16),
        jax.ShapeDtypeStruct((n, QK_W), BF16),
        jax.ShapeDtypeStruct((nb, V_W, rows), BF16),
        jax.ShapeDtypeStruct((n, KV_LORA), F32),
        jax.ShapeDtypeStruct((n, QK_ROPE), F32),
        jax.ShapeDtypeStruct((nb, 2, D_CONV), F32),
    )
    return pl.pallas_call(
        functools.partial(_proj_kernel, tail_row=tail_row),
        grid=(nb, nt),
        in_specs=[
            tok(D_MODEL), _full((1, D_MODEL)), _full((D_MODEL, D_IN_P)), _full((CONV_WIDTH, D_CONV)),
            per_stream, _full((1, Q_LORA)), _full((QK_W, Q_LORA)), _full((1, KV_LORA)),
            _full((QK_W, KV_LORA)), _full((V_W, KV_LORA)), _full((HEAD_PAD, 1)), _full((HEAD_PAD, 1)),
            rope_t, rope_t, rope_t, rope_f, rope_f,
        ],
        out_specs=(tok(D_CONV), tok(D_MODEL), tok(D_MODEL), feat(QK_W), tok(QK_W), feat(V_W),
                   tok(KV_LORA), tok(QK_ROPE), per_stream),
        out_shape=out_shapes,
        scratch_shapes=[pltpu.VMEM((tm + 8, D_CONV), F32)],
        compiler_params=pltpu.CompilerParams(
            dimension_semantics=("arbitrary", "arbitrary"), vmem_limit_bytes=VMEM_LIMIT),
        name="proj",
    )(x, w["mix_norm"], w["w_in"], w["conv_w"], cinit, w["q_a_norm"], w["w_uq_t"], w["kv_a_norm"],
      w["w_uk_t"], w["w_uv_t"], w["qgain_col"], w["kgain_col"], *rope_tok, *rope_feat)


def _attn_prompt_kernel(qt_ref, k_ref, vt_ref, km_ref, vmt_ref, o_ref, s_ref, *, tq, nh):
    seq = k_ref.shape[1]
    nq = seq // tq
    kchunk = lax.broadcasted_iota(jnp.int32, (tq, tq), 0) // CHUNK
    qchunk = lax.broadcasted_iota(jnp.int32, (tq, tq), 1) // CHUNK
    diag_mask = kchunk <= qchunk

    def weighted(vt_blk, p):
        ones = jnp.ones((SUM_ROWS, vt_blk.shape[1]), BF16)
        return _dot(jnp.concatenate([vt_blk, ones], axis=0), p)

    def vrows(hd):
        return slice(hd * V_HEAD, (hd + 1) * V_HEAD)

    def queries(hd, q0):
        return qt_ref[0, _head(hd), pl.ds(q0, tq)]

    def scores(hd, k0, qt):
        s_ref[hd] = _dot(k_ref[0, pl.ds(k0, tq), _head(hd)], qt)

    for hd in range(nh):
        scores(hd, 0, queries(hd, 0))

    def qblock(qi, carry):
        q0 = pl.multiple_of(qi * tq, tq)
        qts = [queries(hd, q0) for hd in range(nh)]

        def body(kj, state):
            k0 = pl.multiple_of(kj * tq, tq)
            out = []
            for hd in range(nh):
                m, acc = state[hd]
                s = s_ref[hd]
                m_new = jnp.maximum(m, jnp.max(s, axis=0, keepdims=True))
                p = jnp.exp2(s - m_new).astype(BF16)
                scores(hd, pl.multiple_of(k0 + tq, tq), qts[hd])
                pv = weighted(vt_ref[0, vrows(hd), pl.ds(k0, tq)], p)
                out.append((m_new, jnp.exp2(m - m_new) * acc + pv))
            return tuple(out)

        empty = (jnp.full((1, tq), NEG_INF, F32), jnp.zeros((V_HEAD + SUM_ROWS, tq), F32))
        state = lax.fori_loop(0, qi, body, (empty,) * nh)

        q_next = pl.multiple_of(jnp.minimum(qi + 1, nq - 1) * tq, tq)
        outs = []
        for hd in range(nh):
            m, acc = state[hd]
            s = jnp.where(diag_mask, s_ref[hd], NEG_INF)
            s_meta = _dot(km_ref[:, _head(hd)], qts[hd])
            m_new = jnp.maximum(m, jnp.maximum(jnp.max(s, axis=0, keepdims=True),
                                               jnp.max(s_meta, axis=0, keepdims=True)))
            p = jnp.exp2(s - m_new).astype(BF16)
            p_meta = jnp.exp2(s_meta - m_new).astype(BF16)
            scores(hd, 0, queries(hd, q_next))
            acc = (jnp.exp2(m - m_new) * acc + weighted(vt_ref[0, vrows(hd), pl.ds(q0, tq)], p)
                   + weighted(vmt_ref[vrows(hd), :], p_meta))
            outs.append(acc[:V_HEAD] / acc[V_HEAD:V_HEAD + 1])
        o_ref[0, pl.ds(q0, tq), :] = jnp.concatenate(outs, axis=0).T.astype(BF16)
        return carry

    lax.fori_loop(0, nq, qblock, 0)


def _attn_prompt(qt, k, vt, k_meta, vt_meta, tq, nh):
    nb, seq, _ = k.shape
    assert seq % tq == 0 and tq % CHUNK == 0 and N_HEADS % nh == 0
    feat = lambda width: pl.BlockSpec((1, width, seq), lambda b, p: (b, p, 0))
    tok = lambda width: pl.BlockSpec((1, seq, width), lambda b, p: (b, 0, p))
    return pl.pallas_call(
        functools.partial(_attn_prompt_kernel, tq=tq, nh=nh),
        grid=(nb, N_HEADS // nh),
        in_specs=[feat(nh * HEAD_PAD), tok(nh * HEAD_PAD), feat(nh * V_HEAD),
                  pl.BlockSpec((N_META, nh * HEAD_PAD), lambda b, p: (0, p)),
                  pl.BlockSpec((nh * V_HEAD, N_META), lambda b, p: (p, 0))],
        out_specs=tok(nh * V_HEAD),
        out_shape=jax.ShapeDtypeStruct((nb, seq, V_W), BF16),
        scratch_shapes=[pltpu.VMEM((nh, tq, tq), F32)],
        compiler_params=pltpu.CompilerParams(
            dimension_semantics=("arbitrary", "arbitrary"), vmem_limit_bytes=VMEM_LIMIT),
        name="attn_prompt",
    )(qt, k, vt, k_meta, vt_meta)


def _attn_sample_kernel(q_ref, k_ref, v_ref, ckv_ref, kpe_ref, wuk_ref, wuv_ref, kgain_ref, o_ref, kc_ref):
    rows = q_ref.shape[0]
    cb = ckv_ref[0].astype(BF16)
    is_a = lax.broadcasted_iota(jnp.int32, (rows, 2 * V_HEAD), 1) < V_HEAD

    def store_k(hd, val):
        kc_ref[hd] = val

    _expand_keys(cb, kpe_ref[0], wuk_ref, kgain_ref[...], store_k)

    for pr in range(N_HEADS // 2):
        vsl = slice(pr * 2 * V_HEAD, (pr + 1) * 2 * V_HEAD)
        v_cache = _dot(cb, wuv_ref[:, vsl]).astype(BF16)
        v_new = v_ref[:, vsl]
        outs = []
        for hh in range(2):
            hd = 2 * pr + hh
            qh = q_ref[:, _head(hd)]
            s1 = _dot_t(qh, kc_ref[hd])
            s2 = _dot_t(qh, k_ref[:, _head(hd)])
            m = jnp.maximum(jnp.max(s1, axis=-1, keepdims=True), jnp.max(s2, axis=-1, keepdims=True))
            p1 = jnp.exp2(s1 - m)
            p2 = jnp.exp2(s2 - m)
            l = jnp.sum(p1, axis=-1, keepdims=True) + jnp.sum(p2, axis=-1, keepdims=True)
            outs.append((_dot(p1.astype(BF16), v_cache) + _dot(p2.astype(BF16), v_new)) / l)
        o_ref[:, vsl] = jnp.where(is_a, outs[0], outs[1]).astype(BF16)


def _attn_sample(q, k, v, ckv_cache, kpe_cache_p, w, nb, rows):
    past = ckv_cache.shape[1]
    tok = lambda width: pl.BlockSpec((rows, width), lambda b: (b, 0))
    return pl.pallas_call(
        _attn_sample_kernel,
        grid=(nb,),
        in_specs=[tok(QK_W), tok(QK_W), tok(V_W),
                  pl.BlockSpec((1, past, KV_LORA), lambda b: (b, 0, 0)),
                  pl.BlockSpec((1, past, HEAD_PAD), lambda b: (b, 0, 0)),
                  _full((KV_LORA, QK_W)), _full((KV_LORA, V_W)), _full((1, HEAD_PAD))],
        out_specs=tok(V_W),
        out_shape=jax.ShapeDtypeStruct((nb * rows, V_W), BF16),
        scratch_shapes=[pltpu.VMEM((N_HEADS, past, HEAD_PAD), BF16)],
        compiler_params=pltpu.CompilerParams(
            dimension_semantics=("arbitrary",), vmem_limit_bytes=VMEM_LIMIT),
        name="attn_sample",
    )(q, k, v, ckv_cache, kpe_cache_p, w["w_uk"], w["w_uv"], w["kgain_row"])


def _merge_kernel(x_ref, bc_ref, at_ref, sgc_ref, sgm_ref, wco_ref, wmo_ref, woa_ref, o_ref):
    merged = (sgc_ref[...].astype(F32) * _dot(bc_ref[...], wco_ref[...])
              + sgm_ref[...].astype(F32) * _dot(at_ref[...], wmo_ref[...]))
    o_ref[...] = x_ref[...] + _dot(merged.astype(BF16), woa_ref[...])


def _merge(x, bc, attn, sgc, sgm, w, tm):
    n = x.shape[0]
    assert n % tm == 0
    tok = pl.BlockSpec((tm, D_MODEL), lambda i: (i, 0))
    sq = _full((D_MODEL, D_MODEL))
    return pl.pallas_call(
        _merge_kernel,
        grid=(n // tm,),
        in_specs=[tok, tok, tok, tok, tok, sq, sq, sq],
        out_specs=tok,
        out_shape=jax.ShapeDtypeStruct((n, D_MODEL), F32),
        compiler_params=pltpu.CompilerParams(
            dimension_semantics=("arbitrary",), vmem_limit_bytes=VMEM_LIMIT),
        name="merge",
    )(x, bc, attn, sgc, sgm, w["w_conv_out"], w["w_mla_out"], w["w_out_all"])


def _rope_tables(pos):
    half = QK_ROPE // 2
    inv_freq = ROPE_THETA ** (-jnp.arange(half, dtype=F32) / half)
    ang = pos.astype(F32)[:, None] * inv_freq[None, :]
    cos, sin = jnp.cos(ang), jnp.sin(ang)
    n = pos.shape[0]
    one = jnp.ones((n, QK_NOPE), F32)
    z16 = jnp.zeros((n, half), F32)
    z32 = jnp.zeros((n, HEAD_PAD - QK_DIM), F32)
    z64 = jnp.zeros((n, QK_NOPE), F32)
    rc = jnp.concatenate([one, cos, cos, z32], axis=1)
    rs1 = jnp.concatenate([z64, -sin, z16, z32], axis=1)
    rs2 = jnp.concatenate([z64, z16, sin, z32], axis=1)
    return (rc, rs1, rs2), (cos.T, sin.T)


def _prep_weights(mix_norm, w_in_all, conv_w, w_conv_out, q_a_norm, w_uq, kv_a_norm, w_ukv, q_norm,
                  k_norm, w_mla_out, w_out_all):
    o_ql = 3 * D_CONV
    o_kv = o_ql + Q_LORA
    o_kpe = o_kv + KV_LORA
    o_gc = o_kpe + QK_ROPE
    zeros = lambda n: jnp.zeros((D_MODEL, n), w_in_all.dtype)
    w_in = jnp.concatenate(
        [w_in_all[:, :o_ql], w_in_all[:, o_gc:], w_in_all[:, o_ql:o_kpe], zeros(QK_NOPE),
         w_in_all[:, o_kpe:o_gc], zeros(HEAD_PAD - QK_DIM)], axis=1).astype(BF16)
    pad_head = lambda a: jnp.pad(a, ((0, 0), (0, 0), (0, HEAD_PAD - a.shape[-1]))).reshape(a.shape[0], QK_W)
    w_ukv3 = w_ukv.reshape(KV_LORA, N_HEADS, QK_NOPE + V_HEAD)
    pad_gain = lambda g: jnp.pad(g, (0, HEAD_PAD - QK_DIM)).astype(F32)
    w_uk = pad_head(w_ukv3[..., :QK_NOPE]).astype(BF16)
    w_uv = w_ukv3[..., QK_NOPE:].reshape(KV_LORA, V_W).astype(BF16)
    kgain = pad_gain(k_norm)
    return {
        "mix_norm": mix_norm.reshape(1, D_MODEL),
        "w_in": w_in,
        "conv_w": conv_w,
        "q_a_norm": q_a_norm.reshape(1, Q_LORA),
        "w_uq_t": pad_head(w_uq.reshape(Q_LORA, N_HEADS, QK_DIM)).astype(BF16).T,
        "kv_a_norm": kv_a_norm.reshape(1, KV_LORA),
        "w_uk": w_uk,
        "w_uk_t": w_uk.T,
        "w_uv": w_uv,
        "w_uv_t": w_uv.T,
        "qgain_col": pad_gain(q_norm * (QK_DIM ** -0.5 * LOG2E)).reshape(HEAD_PAD, 1),
        "kgain_col": kgain.reshape(HEAD_PAD, 1),
        "kgain_row": kgain.reshape(1, HEAD_PAD),
        "w_conv_out": w_conv_out.astype(BF16),
        "w_mla_out": w_mla_out.astype(BF16),
        "w_out_all": w_out_all.astype(BF16),
    }


def kernel(x_prompt, x_sample, cache_conv, cache_kv_latent, cache_k_rope, meta_tokens, ffn1_norm, ffn1_w_gate, ffn1_w_up, ffn1_w_down, mix_norm, w_in_all, conv_w, w_conv_out, q_a_norm, w_uq, kv_a_norm, w_ukv, q_norm, k_norm, w_mla_out, w_out_all, ffn2_norm, ffn2_w_gate, ffn2_w_up, ffn2_w_down):
    depth = ffn1_norm.shape[0]
    assert depth == 1
    nb, seq, _ = x_prompt.shape
    db, dseq, _ = x_sample.shape
    past = cache_kv_latent.shape[2]
    assert dseq == 2 * N_META and seq % 512 == 0

    w = _prep_weights(mix_norm[0], w_in_all[0], conv_w[0], w_conv_out[0], q_a_norm[0], w_uq[0],
                      kv_a_norm[0], w_ukv[0], q_norm[0], k_norm[0], w_mla_out[0], w_out_all[0])
    ffn1 = (ffn1_norm[0].reshape(1, D_MODEL), ffn1_w_gate[0].astype(BF16), ffn1_w_up[0].astype(BF16),
            ffn1_w_down[0].astype(BF16))
    ffn2 = (ffn2_norm[0].reshape(1, D_MODEL), ffn2_w_gate[0].astype(BF16), ffn2_w_up[0].astype(BF16),
            ffn2_w_down[0].astype(BF16))

    n_s = db * dseq
    ns = db + 1
    xs = jnp.concatenate([x_sample.reshape(n_s, D_MODEL), jnp.zeros((dseq - N_META, D_MODEL), F32),
                          meta_tokens.astype(F32)], axis=0)
    xf = x_prompt.reshape(nb * seq, D_MODEL)

    x1s = _ffn(xs, *ffn1, tm=xs.shape[0])
    x1f = _ffn(xf, *ffn1, tm=512)

    pad_rows = lambda a: jnp.pad(a.reshape(ns, dseq, -1), ((0, 0), (0, SMALL_ROWS - dseq), (0, 0)))
    pos_s = N_META + past + jnp.arange(dseq, dtype=jnp.int32)
    pos_m = jnp.concatenate([jnp.zeros((dseq - N_META,), jnp.int32), jnp.arange(N_META, dtype=jnp.int32)])
    pos_small = jnp.pad(jnp.concatenate([jnp.tile(pos_s, db), pos_m]).reshape(ns, dseq),
                        ((0, 0), (0, SMALL_ROWS - dseq))).reshape(-1)
    rope_s = _rope_tables(pos_small)
    rope_f = _rope_tables(N_META + jnp.arange(seq, dtype=jnp.int32))

    cinit_s = jnp.concatenate([cache_conv[0].astype(F32), jnp.zeros((1, 2, D_CONV), F32)], axis=0)
    (bc_s, sgc_s, sgm_s, qt_s, k_s, vt_s, ckv_s, kpe_s, tail_s) = _proj(
        pad_rows(x1s).reshape(ns * SMALL_ROWS, D_MODEL), cinit_s, *rope_s, True, w, ns, SMALL_ROWS,
        SMALL_ROWS, dseq)
    cinit_f = jnp.broadcast_to(tail_s[db:db + 1], (nb, 2, D_CONV))
    (bc_f, sgc_f, sgm_f, qt_f, k_f, vt_f, ckv_f, kpe_f, tail_f) = _proj(
        x1f, cinit_f, *rope_f, False, w, nb, seq, 512, 512)

    unpad = lambda a: a.reshape(ns, SMALL_ROWS, -1)[:, :dseq]
    k_s3 = unpad(k_s)
    attn_f = _attn_prompt(qt_f, k_f.reshape(nb, seq, QK_W), vt_f, k_s3[db, dseq - N_META:],
                          vt_s[db, :, dseq - N_META:dseq], tq=512, nh=4)

    kpe_cache_p = jnp.pad(cache_k_rope[0].astype(F32), ((0, 0), (0, 0), (QK_NOPE, HEAD_PAD - QK_DIM)))
    q_s = qt_s[:db, :, :dseq].transpose(0, 2, 1).reshape(n_s, QK_W)
    v_s = vt_s[:db, :, :dseq].transpose(0, 2, 1).reshape(n_s, V_W)
    attn_s = _attn_sample(q_s, k_s3[:db].reshape(n_s, QK_W), v_s, cache_kv_latent[0].astype(F32),
                          kpe_cache_p, w, db, dseq)

    x2f = _merge(x1f, bc_f, attn_f.reshape(nb * seq, V_W), sgc_f, sgm_f, w, 512)
    sample_rows = lambda a: unpad(a)[:db].reshape(n_s, -1)
    x2s = _merge(x1s[:n_s], sample_rows(bc_s), attn_s, sample_rows(sgc_s), sample_rows(sgm_s), w, n_s)

    y_prompt = _ffn(x2f, *ffn2, tm=512).reshape(nb, seq, D_MODEL)
    y_sample = _ffn(x2s, *ffn2, tm=n_s).reshape(db, dseq, D_MODEL)

    ckv_s3, kpe_s3 = unpad(ckv_s), unpad(kpe_s)
    meta_rows = lambda a: jnp.broadcast_to(a[db, dseq - N_META:][None], (nb, N_META, a.shape[-1]))
    new_kv_p = jnp.concatenate([meta_rows(ckv_s3), ckv_f.reshape(nb, seq, KV_LORA)], axis=1)
    new_kpe_p = jnp.concatenate([meta_rows(kpe_s3), kpe_f.reshape(nb, seq, QK_ROPE)], axis=1)
    return (y_prompt, y_sample, tail_f[None], new_kv_p[None], new_kpe_p[None], tail_s[:db][None],
            ckv_s3[:db][None], kpe_s3[:db][None])
```

```python
import functools

import jax
import jax.numpy as jnp
from jax import lax
from jax.experimental import pallas as pl
from jax.experimental.pallas import tpu as pltpu

D_MODEL = 1024
D_FF = 2816
D_CONV = 1024
CONV_WIDTH = 3
N_HEADS = 16
QK_NOPE = 64
QK_ROPE = 32
QK_DIM = QK_NOPE + QK_ROPE
V_HEAD = 64
Q_LORA = 384
KV_LORA = 128
N_META = 16
CHUNK = 64
ROPE_THETA = 10000.0
RMS_EPS = 1e-6
NEG_INF = -1e30

HEAD_PAD = 128
QK_W = N_HEADS * HEAD_PAD
V_W = N_HEADS * V_HEAD
SMALL_ROWS = 128
SUM_ROWS = 16
LOG2E = 1.4426950408889634
OFF_B, OFF_C, OFF_V, OFF_GC, OFF_GM = 0, 1024, 2048, 3072, 4096
OFF_QL = 5120
OFF_KV = OFF_QL + Q_LORA
OFF_KPE = OFF_KV + KV_LORA
D_IN_P = OFF_KPE + HEAD_PAD

VMEM_LIMIT = 58 * 1024 * 1024

F32 = jnp.float32
BF16 = jnp.bfloat16


def _rms(x, g):
    return x * lax.rsqrt(jnp.mean(x * x, axis=-1, keepdims=True) + RMS_EPS) * g


def _dot(a, b):
    return jnp.dot(a, b, preferred_element_type=F32)


def _dot_t(a, b):
    return lax.dot_general(a, b, (((1,), (1,)), ((), ())), preferred_element_type=F32)


def _full(shape):
    return pl.BlockSpec(shape, lambda *_: (0,) * len(shape))


def _head(hd):
    return slice(hd * HEAD_PAD, (hd + 1) * HEAD_PAD)


def _ffn_kernel(x_ref, g_ref, wg_ref, wu_ref, wd_ref, o_ref):
    x = x_ref[...]
    h = _rms(x, g_ref[...]).astype(BF16)
    gate = _dot(h, wg_ref[...])
    up = _dot(h, wu_ref[...])
    a = (gate * jax.nn.sigmoid(gate) * up).astype(BF16)
    o_ref[...] = x + 0.5 * _dot(a, wd_ref[...])


def _ffn(x, g, wg, wu, wd, tm):
    n = x.shape[0]
    assert n % tm == 0
    return pl.pallas_call(
        _ffn_kernel,
        grid=(n // tm,),
        in_specs=[
            pl.BlockSpec((tm, D_MODEL), lambda i: (i, 0)),
            _full((1, D_MODEL)),
            _full((D_MODEL, D_FF)),
            _full((D_MODEL, D_FF)),
            _full((D_FF, D_MODEL)),
        ],
        out_specs=pl.BlockSpec((tm, D_MODEL), lambda i: (i, 0)),
        out_shape=jax.ShapeDtypeStruct((n, D_MODEL), F32),
        compiler_params=pltpu.CompilerParams(
            dimension_semantics=("arbitrary",), vmem_limit_bytes=VMEM_LIMIT),
        name="ffn",
    )(x, g, wg, wu, wd)


def _expand_keys(cb, kpr, wuk_ref, kgain, store):
    sspe = jnp.sum(kpr * kpr, axis=-1, keepdims=True)
    for hd in range(N_HEADS):
        kn = _dot(cb, wuk_ref[:, _head(hd)])
        ss = (jnp.sum(kn * kn, axis=-1, keepdims=True) + sspe) * (1.0 / QK_DIM)
        store(hd, ((kn + kpr) * lax.rsqrt(ss + RMS_EPS) * kgain).astype(BF16))


def _proj_kernel(x_ref, g_ref, win_ref, cw_ref, cinit_ref, qag_ref, wuqt_ref, kvag_ref, wukt_ref,
                 wuvt_ref, qgain_ref, kgain_ref, rc_ref, rs1_ref, rs2_ref, cost_ref, sint_ref,
                 bc_ref, sgc_ref, sgm_ref, qt_ref, k_ref, vt_ref, ckv_ref, kpe_ref, tail_ref, cbuf,
                 *, tail_row):
    tm = x_ref.shape[0]
    half = QK_ROPE // 2

    @pl.when(pl.program_id(1) == 0)
    def _():
        cbuf[6:8, :] = cinit_ref[0]

    h = _rms(x_ref[...], g_ref[...]).astype(BF16)

    def proj(lo, n):
        return _dot(h, win_ref[:, lo:lo + n])

    cin = proj(OFF_C, D_CONV) * proj(OFF_V, D_CONV)
    cbuf[8:8 + tm, :] = cin
    cw = cw_ref[...]
    y = cw[0:1] * cbuf[6:6 + tm, :] + cw[1:2] * cbuf[7:7 + tm, :] + cw[2:3] * cin
    bc_ref[...] = (proj(OFF_B, D_CONV) * y).astype(BF16)
    tail = cbuf[tail_row + 6:tail_row + 8, :]
    cbuf[6:8, :] = tail
    tail_ref[0] = tail

    sgc_ref[...] = jax.nn.sigmoid(proj(OFF_GC, D_MODEL)).astype(BF16)
    sgm_ref[...] = jax.nn.sigmoid(proj(OFF_GM, D_MODEL)).astype(BF16)

    qn = _rms(proj(OFF_QL, Q_LORA), qag_ref[...]).astype(BF16)
    qt = _dot_t(wuqt_ref[...], qn)
    cos_t, sin_t = cost_ref[...], sint_ref[...]
    qgain = jnp.broadcast_to(qgain_ref[...], (HEAD_PAD, tm))
    for hd in range(N_HEADS):
        r0 = hd * HEAD_PAD
        x1 = qt[r0 + QK_NOPE:r0 + QK_NOPE + half]
        x2 = qt[r0 + QK_NOPE + half:r0 + QK_DIM]
        rot = jnp.concatenate([qt[r0:r0 + QK_NOPE], x1 * cos_t - x2 * sin_t, x2 * cos_t + x1 * sin_t,
                               qt[r0 + QK_DIM:r0 + HEAD_PAD]], axis=0)
        ss = jnp.sum(rot * rot, axis=0, keepdims=True) * (1.0 / QK_DIM)
        qt_ref[0, _head(hd), :] = (rot * lax.rsqrt(ss + RMS_EPS) * qgain).astype(BF16)

    ckv = _rms(proj(OFF_KV, KV_LORA), kvag_ref[...])
    ckv_ref[...] = ckv
    cb = ckv.astype(BF16)
    kpb = proj(OFF_KPE, HEAD_PAD)
    kpr = (kpb * rc_ref[...] + pltpu.roll(kpb, HEAD_PAD - half, 1) * rs1_ref[...]
           + pltpu.roll(kpb, half, 1) * rs2_ref[...])
    kpe_ref[...] = kpr[:, QK_NOPE:QK_DIM]

    vt_ref[0] = _dot_t(wuvt_ref[...], cb).astype(BF16)
    knt = _dot_t(wukt_ref[...], cb)
    kprt = kpr.T
    sspe = jnp.sum(kprt * kprt, axis=0, keepdims=True)
    kgain = jnp.broadcast_to(kgain_ref[...], (HEAD_PAD, tm))
    for hd in range(N_HEADS):
        kn = knt[_head(hd)]
        ss = (jnp.sum(kn * kn, axis=0, keepdims=True) + sspe) * (1.0 / QK_DIM)
        kt = (kn + kprt) * lax.rsqrt(ss + RMS_EPS) * kgain
        k_ref[:, _head(hd)] = kt.T.astype(BF16)


def _proj(x, cinit, rope_tok, rope_feat, rope_per_stream, w, nb, rows, tm, tail_row):
    nt = rows // tm
    assert rows % tm == 0 and tm % 128 == 0
    n = nb * rows
    tok = lambda width: pl.BlockSpec((tm, width), lambda b, j: (b * nt + j, 0))
    feat = lambda width: pl.BlockSpec((1, width, tm), lambda b, j: (b, 0, j))
    if rope_per_stream:
        rope_t = pl.BlockSpec((tm, HEAD_PAD), lambda b, j: (b * nt + j, 0))
        rope_f = pl.BlockSpec((QK_ROPE // 2, tm), lambda b, j: (0, b * nt + j))
    else:
        rope_t = pl.BlockSpec((tm, HEAD_PAD), lambda b, j: (j, 0))
        rope_f = pl.BlockSpec((QK_ROPE // 2, tm), lambda b, j: (0, j))
    per_stream = pl.BlockSpec((1, 2, D_CONV), lambda b, j: (b, 0, 0))
    out_shapes = (
        jax.ShapeDtypeStruct((n, D_CONV), BF16),
        jax.ShapeDtypeStruct((n, D_MODEL), BF16),
        jax.ShapeDtypeStruct((n, D_MODEL), BF16),
        jax.ShapeDtypeStruct((nb, QK_W, rows), BF16),
        jax.ShapeDtypeStruct((n, QK_W), BF16),
        jax.ShapeDtypeStruct((nb, V_W, rows), BF16),
        jax.ShapeDtypeStruct((n, KV_LORA), F32),
        jax.ShapeDtypeStruct((n, QK_ROPE), F32),
        jax.ShapeDtypeStruct((nb, 2, D_CONV), F32),
    )
    return pl.pallas_call(
        functools.partial(_proj_kernel, tail_row=tail_row),
        grid=(nb, nt),
        in_specs=[
            tok(D_MODEL), _full((1, D_MODEL)), _full((D_MODEL, D_IN_P)), _full((CONV_WIDTH, D_CONV)),
            per_stream, _full((1, Q_LORA)), _full((QK_W, Q_LORA)), _full((1, KV_LORA)),
            _full((QK_W, KV_LORA)), _full((V_W, KV_LORA)), _full((HEAD_PAD, 1)), _full((HEAD_PAD, 1)),
            rope_t, rope_t, rope_t, rope_f, rope_f,
        ],
        out_specs=(tok(D_CONV), tok(D_MODEL), tok(D_MODEL), feat(QK_W), tok(QK_W), feat(V_W),
                   tok(KV_LORA), tok(QK_ROPE), per_stream),
        out_shape=out_shapes,
        scratch_shapes=[pltpu.VMEM((tm + 8, D_CONV), F32)],
        compiler_params=pltpu.CompilerParams(
            dimension_semantics=("arbitrary", "arbitrary"), vmem_limit_bytes=VMEM_LIMIT),
        name="proj",
    )(x, w["mix_norm"], w["w_in"], w["conv_w"], cinit, w["q_a_norm"], w["w_uq_t"], w["kv_a_norm"],
      w["w_uk_t"], w["w_uv_t"], w["qgain_col"], w["kgain_col"], *rope_tok, *rope_feat)


def _attn_prompt_kernel(qt_ref, k_ref, vt_ref, km_ref, vmt_ref, o_ref, s_ref, *, tq, tk, nh):
    seq = k_ref.shape[1]
    nq = seq // tq
    nsub = tq // tk
    units = [(hd, sub) for sub in range(nsub) for hd in range(nh)]
    kchunk = lax.broadcasted_iota(jnp.int32, (tk, tq), 0) // CHUNK
    qchunk = lax.broadcasted_iota(jnp.int32, (tk, tq), 1) // CHUNK
    diag_masks = [kchunk + sub * (tk // CHUNK) <= qchunk for sub in range(nsub)]

    def weighted(vt_blk, p):
        ones = jnp.ones((SUM_ROWS, vt_blk.shape[1]), BF16)
        return _dot(jnp.concatenate([vt_blk, ones], axis=0), p)

    def vrows(hd):
        return slice(hd * V_HEAD, (hd + 1) * V_HEAD)

    def queries(hd, q0):
        return qt_ref[0, _head(hd), pl.ds(q0, tq)]

    def scores(hd, sub, k0, qt):
        s = _dot(k_ref[0, pl.ds(k0 + sub * tk, tk), _head(hd)], qt)
        s_ref[hd, sub] = s
        return jnp.max(s, axis=0, keepdims=True)

    def probs(s, m_new):
        return jnp.exp2((s - m_new).astype(BF16))

    def qblock(qi, cmax):
        q0 = pl.multiple_of(qi * tq, tq)
        qts = [queries(hd, q0) for hd in range(nh)]

        def body(kj, carry):
            state, cmax = list(carry[0]), list(carry[1])
            k0 = pl.multiple_of(kj * tq, tq)
            for u, (hd, sub) in enumerate(units):
                m, acc = state[hd]
                m_new = jnp.maximum(m, cmax[u])
                p = probs(s_ref[hd, sub], m_new)
                cmax[u] = scores(hd, sub, pl.multiple_of(k0 + tq, tq), qts[hd])
                pv = weighted(vt_ref[0, vrows(hd), pl.ds(k0 + sub * tk, tk)], p)
                state[hd] = (m_new, jnp.exp2(m - m_new) * acc + pv)
            return tuple(state), tuple(cmax)

        empty = (jnp.full((1, tq), NEG_INF, F32), jnp.zeros((V_HEAD + SUM_ROWS, tq), F32))
        state, _ = lax.fori_loop(0, qi, body, ((empty,) * nh, cmax))
        state = list(state)

        q_next = pl.multiple_of(jnp.minimum(qi + 1, nq - 1) * tq, tq)
        cmax = [None] * len(units)
        for u, (hd, sub) in enumerate(units):
            m, acc = state[hd]
            s = jnp.where(diag_masks[sub], s_ref[hd, sub], NEG_INF)
            m_new = jnp.maximum(m, jnp.max(s, axis=0, keepdims=True))
            if sub == 0:
                s_meta = _dot(km_ref[:, _head(hd)], qts[hd])
                m_new = jnp.maximum(m_new, jnp.max(s_meta, axis=0, keepdims=True))
            p = probs(s, m_new)
            cmax[u] = scores(hd, sub, 0, queries(hd, q_next))
            pv = weighted(vt_ref[0, vrows(hd), pl.ds(q0 + sub * tk, tk)], p)
            if sub == 0:
                pv = pv + weighted(vmt_ref[vrows(hd), :], probs(s_meta, m_new))
            state[hd] = (m_new, jnp.exp2(m - m_new) * acc + pv)
        outs = [acc[:V_HEAD] / acc[V_HEAD:V_HEAD + 1] for _, acc in state]
        o_ref[0, pl.ds(q0, tq), :] = jnp.concatenate(outs, axis=0).T.astype(BF16)
        return tuple(cmax)

    cmax0 = tuple(scores(hd, sub, 0, queries(hd, 0)) for hd, sub in units)
    lax.fori_loop(0, nq, qblock, cmax0)


def _attn_prompt(qt, k, vt, k_meta, vt_meta, tq, tk, nh):
    nb, seq, _ = k.shape
    assert seq % tq == 0 and tq % tk == 0 and tk % CHUNK == 0 and N_HEADS % nh == 0
    feat = lambda width: pl.BlockSpec((1, width, seq), lambda b, p: (b, p, 0))
    tok = lambda width: pl.BlockSpec((1, seq, width), lambda b, p: (b, 0, p))
    return pl.pallas_call(
        functools.partial(_attn_prompt_kernel, tq=tq, tk=tk, nh=nh),
        grid=(nb, N_HEADS // nh),
        in_specs=[feat(nh * HEAD_PAD), tok(nh * HEAD_PAD), feat(nh * V_HEAD),
                  pl.BlockSpec((N_META, nh * HEAD_PAD), lambda b, p: (0, p)),
                  pl.BlockSpec((nh * V_HEAD, N_META), lambda b, p: (p, 0))],
        out_specs=tok(nh * V_HEAD),
        out_shape=jax.ShapeDtypeStruct((nb, seq, V_W), BF16),
        scratch_shapes=[pltpu.VMEM((nh, tq // tk, tk, tq), F32)],
        compiler_params=pltpu.CompilerParams(
            dimension_semantics=("arbitrary", "arbitrary"), vmem_limit_bytes=VMEM_LIMIT),
        name="attn_prompt",
    )(qt, k, vt, k_meta, vt_meta)


def _attn_sample_kernel(q_ref, k_ref, v_ref, ckv_ref, kpe_ref, wuk_ref, wuv_ref, kgain_ref, o_ref, kc_ref):
    rows = q_ref.shape[0]
    cb = ckv_ref[0].astype(BF16)
    is_a = lax.broadcasted_iota(jnp.int32, (rows, 2 * V_HEAD), 1) < V_HEAD

    def store_k(hd, val):
        kc_ref[hd] = val

    _expand_keys(cb, kpe_ref[0], wuk_ref, kgain_ref[...], store_k)

    for pr in range(N_HEADS // 2):
        vsl = slice(pr * 2 * V_HEAD, (pr + 1) * 2 * V_HEAD)
        v_cache = _dot(cb, wuv_ref[:, vsl]).astype(BF16)
        v_new = v_ref[:, vsl]
        outs = []
        for hh in range(2):
            hd = 2 * pr + hh
            qh = q_ref[:, _head(hd)]
            s1 = _dot_t(qh, kc_ref[hd])
            s2 = _dot_t(qh, k_ref[:, _head(hd)])
            m = jnp.maximum(jnp.max(s1, axis=-1, keepdims=True), jnp.max(s2, axis=-1, keepdims=True))
            p1 = jnp.exp2(s1 - m)
            p2 = jnp.exp2(s2 - m)
            l = jnp.sum(p1, axis=-1, keepdims=True) + jnp.sum(p2, axis=-1, keepdims=True)
            outs.append((_dot(p1.astype(BF16), v_cache) + _dot(p2.astype(BF16), v_new)) / l)
        o_ref[:, vsl] = jnp.where(is_a, outs[0], outs[1]).astype(BF16)


def _attn_sample(q, k, v, ckv_cache, kpe_cache_p, w, nb, rows):
    past = ckv_cache.shape[1]
    tok = lambda width: pl.BlockSpec((rows, width), lambda b: (b, 0))
    return pl.pallas_call(
        _attn_sample_kernel,
        grid=(nb,),
        in_specs=[tok(QK_W), tok(QK_W), tok(V_W),
                  pl.BlockSpec((1, past, KV_LORA), lambda b: (b, 0, 0)),
                  pl.BlockSpec((1, past, HEAD_PAD), lambda b: (b, 0, 0)),
                  _full((KV_LORA, QK_W)), _full((KV_LORA, V_W)), _full((1, HEAD_PAD))],
        out_specs=tok(V_W),
        out_shape=jax.ShapeDtypeStruct((nb * rows, V_W), BF16),
        scratch_shapes=[pltpu.VMEM((N_HEADS, past, HEAD_PAD), BF16)],
        compiler_params=pltpu.CompilerParams(
            dimension_semantics=("arbitrary",), vmem_limit_bytes=VMEM_LIMIT),
        name="attn_sample",
    )(q, k, v, ckv_cache, kpe_cache_p, w["w_uk"], w["w_uv"], w["kgain_row"])


def _merge_kernel(x_ref, bc_ref, at_ref, sgc_ref, sgm_ref, wco_ref, wmo_ref, woa_ref, o_ref):
    merged = (sgc_ref[...].astype(F32) * _dot(bc_ref[...], wco_ref[...])
              + sgm_ref[...].astype(F32) * _dot(at_ref[...], wmo_ref[...]))
    o_ref[...] = x_ref[...] + _dot(merged.astype(BF16), woa_ref[...])


def _merge(x, bc, attn, sgc, sgm, w, tm):
    n = x.shape[0]
    assert n % tm == 0
    tok = pl.BlockSpec((tm, D_MODEL), lambda i: (i, 0))
    sq = _full((D_MODEL, D_MODEL))
    return pl.pallas_call(
        _merge_kernel,
        grid=(n // tm,),
        in_specs=[tok, tok, tok, tok, tok, sq, sq, sq],
        out_specs=tok,
        out_shape=jax.ShapeDtypeStruct((n, D_MODEL), F32),
        compiler_params=pltpu.CompilerParams(
            dimension_semantics=("arbitrary",), vmem_limit_bytes=VMEM_LIMIT),
        name="merge",
    )(x, bc, attn, sgc, sgm, w["w_conv_out"], w["w_mla_out"], w["w_out_all"])


def _rope_tables(pos):
    half = QK_ROPE // 2
    inv_freq = ROPE_THETA ** (-jnp.arange(half, dtype=F32) / half)
    ang = pos.astype(F32)[:, None] * inv_freq[None, :]
    cos, sin = jnp.cos(ang), jnp.sin(ang)
    n = pos.shape[0]
    one = jnp.ones((n, QK_NOPE), F32)
    z16 = jnp.zeros((n, half), F32)
    z32 = jnp.zeros((n, HEAD_PAD - QK_DIM), F32)
    z64 = jnp.zeros((n, QK_NOPE), F32)
    rc = jnp.concatenate([one, cos, cos, z32], axis=1)
    rs1 = jnp.concatenate([z64, -sin, z16, z32], axis=1)
    rs2 = jnp.concatenate([z64, z16, sin, z32], axis=1)
    return (rc, rs1, rs2), (cos.T, sin.T)


def _prep_weights(mix_norm, w_in_all, conv_w, w_conv_out, q_a_norm, w_uq, kv_a_norm, w_ukv, q_norm,
                  k_norm, w_mla_out, w_out_all):
    o_ql = 3 * D_CONV
    o_kv = o_ql + Q_LORA
    o_kpe = o_kv + KV_LORA
    o_gc = o_kpe + QK_ROPE
    zeros = lambda n: jnp.zeros((D_MODEL, n), w_in_all.dtype)
    w_in = jnp.concatenate(
        [w_in_all[:, :o_ql], w_in_all[:, o_gc:], w_in_all[:, o_ql:o_kpe], zeros(QK_NOPE),
         w_in_all[:, o_kpe:o_gc], zeros(HEAD_PAD - QK_DIM)], axis=1).astype(BF16)
    pad_head = lambda a: jnp.pad(a, ((0, 0), (0, 0), (0, HEAD_PAD - a.shape[-1]))).reshape(a.shape[0], QK_W)
    w_ukv3 = w_ukv.reshape(KV_LORA, N_HEADS, QK_NOPE + V_HEAD)
    pad_gain = lambda g: jnp.pad(g, (0, HEAD_PAD - QK_DIM)).astype(F32)
    w_uk = pad_head(w_ukv3[..., :QK_NOPE]).astype(BF16)
    w_uv = w_ukv3[..., QK_NOPE:].reshape(KV_LORA, V_W).astype(BF16)
    kgain = pad_gain(k_norm)
    return {
        "mix_norm": mix_norm.reshape(1, D_MODEL),
        "w_in": w_in,
        "conv_w": conv_w,
        "q_a_norm": q_a_norm.reshape(1, Q_LORA),
        "w_uq_t": pad_head(w_uq.reshape(Q_LORA, N_HEADS, QK_DIM)).astype(BF16).T,
        "kv_a_norm": kv_a_norm.reshape(1, KV_LORA),
        "w_uk": w_uk,
        "w_uk_t": w_uk.T,
        "w_uv": w_uv,
        "w_uv_t": w_uv.T,
        "qgain_col": pad_gain(q_norm * (QK_DIM ** -0.5 * LOG2E)).reshape(HEAD_PAD, 1),
        "kgain_col": kgain.reshape(HEAD_PAD, 1),
        "kgain_row": kgain.reshape(1, HEAD_PAD),
        "w_conv_out": w_conv_out.astype(BF16),
        "w_mla_out": w_mla_out.astype(BF16),
        "w_out_all": w_out_all.astype(BF16),
    }


def kernel(x_prompt, x_sample, cache_conv, cache_kv_latent, cache_k_rope, meta_tokens, ffn1_norm, ffn1_w_gate, ffn1_w_up, ffn1_w_down, mix_norm, w_in_all, conv_w, w_conv_out, q_a_norm, w_uq, kv_a_norm, w_ukv, q_norm, k_norm, w_mla_out, w_out_all, ffn2_norm, ffn2_w_gate, ffn2_w_up, ffn2_w_down):
    depth = ffn1_norm.shape[0]
    assert depth == 1
    nb, seq, _ = x_prompt.shape
    db, dseq, _ = x_sample.shape
    past = cache_kv_latent.shape[2]
    assert dseq == 2 * N_META and seq % 512 == 0

    w = _prep_weights(mix_norm[0], w_in_all[0], conv_w[0], w_conv_out[0], q_a_norm[0], w_uq[0],
                      kv_a_norm[0], w_ukv[0], q_norm[0], k_norm[0], w_mla_out[0], w_out_all[0])
    ffn1 = (ffn1_norm[0].reshape(1, D_MODEL), ffn1_w_gate[0].astype(BF16), ffn1_w_up[0].astype(BF16),
            ffn1_w_down[0].astype(BF16))
    ffn2 = (ffn2_norm[0].reshape(1, D_MODEL), ffn2_w_gate[0].astype(BF16), ffn2_w_up[0].astype(BF16),
            ffn2_w_down[0].astype(BF16))

    n_s = db * dseq
    ns = db + 1
    xs = jnp.concatenate([x_sample.reshape(n_s, D_MODEL), jnp.zeros((dseq - N_META, D_MODEL), F32),
                          meta_tokens.astype(F32)], axis=0)
    xf = x_prompt.reshape(nb * seq, D_MODEL)

    x1s = _ffn(xs, *ffn1, tm=xs.shape[0])
    x1f = _ffn(xf, *ffn1, tm=512)

    pad_rows = lambda a: jnp.pad(a.reshape(ns, dseq, -1), ((0, 0), (0, SMALL_ROWS - dseq), (0, 0)))
    pos_s = N_META + past + jnp.arange(dseq, dtype=jnp.int32)
    pos_m = jnp.concatenate([jnp.zeros((dseq - N_META,), jnp.int32), jnp.arange(N_META, dtype=jnp.int32)])
    pos_small = jnp.pad(jnp.concatenate([jnp.tile(pos_s, db), pos_m]).reshape(ns, dseq),
                        ((0, 0), (0, SMALL_ROWS - dseq))).reshape(-1)
    rope_s = _rope_tables(pos_small)
    rope_f = _rope_tables(N_META + jnp.arange(seq, dtype=jnp.int32))

    cinit_s = jnp.concatenate([cache_conv[0].astype(F32), jnp.zeros((1, 2, D_CONV), F32)], axis=0)
    (bc_s, sgc_s, sgm_s, qt_s, k_s, vt_s, ckv_s, kpe_s, tail_s) = _proj(
        pad_rows(x1s).reshape(ns * SMALL_ROWS, D_MODEL), cinit_s, *rope_s, True, w, ns, SMALL_ROWS,
        SMALL_ROWS, dseq)
    cinit_f = jnp.broadcast_to(tail_s[db:db + 1], (nb, 2, D_CONV))
    (bc_f, sgc_f, sgm_f, qt_f, k_f, vt_f, ckv_f, kpe_f, tail_f) = _proj(
        x1f, cinit_f, *rope_f, False, w, nb, seq, 512, 512)

    unpad = lambda a: a.reshape(ns, SMALL_ROWS, -1)[:, :dseq]
    k_s3 = unpad(k_s)
    attn_f = _attn_prompt(qt_f, k_f.reshape(nb, seq, QK_W), vt_f, k_s3[db, dseq - N_META:],
                          vt_s[db, :, dseq - N_META:dseq], tq=512, tk=512, nh=4)

    kpe_cache_p = jnp.pad(cache_k_rope[0].astype(F32), ((0, 0), (0, 0), (QK_NOPE, HEAD_PAD - QK_DIM)))
    q_s = qt_s[:db, :, :dseq].transpose(0, 2, 1).reshape(n_s, QK_W)
    v_s = vt_s[:db, :, :dseq].transpose(0, 2, 1).reshape(n_s, V_W)
    attn_s = _attn_sample(q_s, k_s3[:db].reshape(n_s, QK_W), v_s, cache_kv_latent[0].astype(F32),
                          kpe_cache_p, w, db, dseq)

    x2f = _merge(x1f, bc_f, attn_f.reshape(nb * seq, V_W), sgc_f, sgm_f, w, 512)
    sample_rows = lambda a: unpad(a)[:db].reshape(n_s, -1)
    x2s = _merge(x1s[:n_s], sample_rows(bc_s), attn_s, sample_rows(sgc_s), sample_rows(sgm_s), w, n_s)

    y_prompt = _ffn(x2f, *ffn2, tm=512).reshape(nb, seq, D_MODEL)
    y_sample = _ffn(x2s, *ffn2, tm=n_s).reshape(db, dseq, D_MODEL)

    ckv_s3, kpe_s3 = unpad(ckv_s), unpad(kpe_s)
    meta_rows = lambda a: jnp.broadcast_to(a[db, dseq - N_META:][None], (nb, N_META, a.shape[-1]))
    new_kv_p = jnp.concatenate([meta_rows(ckv_s3), ckv_f.reshape(nb, seq, KV_LORA)], axis=1)
    new_kpe_p = jnp.concatenate([meta_rows(kpe_s3), kpe_f.reshape(nb, seq, QK_ROPE)], axis=1)
    return (y_prompt, y_sample, tail_f[None], new_kv_p[None], new_kpe_p[None], tail_s[:db][None],
            ckv_s3[:db][None], kpe_s3[:db][None])
```

```python
import functools

import jax
import jax.numpy as jnp
from jax import lax
from jax.experimental import pallas as pl
from jax.experimental.pallas import tpu as pltpu

D_MODEL = 1024
D_FF = 2816
D_CONV = 1024
CONV_WIDTH = 3
N_HEADS = 16
QK_NOPE = 64
QK_ROPE = 32
QK_DIM = QK_NOPE + QK_ROPE
V_HEAD = 64
Q_LORA = 384
KV_LORA = 128
N_META = 16
CHUNK = 64
ROPE_THETA = 10000.0
RMS_EPS = 1e-6
NEG_INF = -1e30

HEAD_PAD = 128
QK_W = N_HEADS * HEAD_PAD
V_W = N_HEADS * V_HEAD
SMALL_ROWS = 128
SUM_ROWS = 16
LOG2E = 1.4426950408889634
ATTN_TQ = 512
OFF_B, OFF_C, OFF_V, OFF_GC, OFF_GM = 0, 1024, 2048, 3072, 4096
OFF_QL = 5120
OFF_KV = OFF_QL + Q_LORA
OFF_KPE = OFF_KV + KV_LORA
D_IN_P = OFF_KPE + HEAD_PAD

VMEM_LIMIT = 58 * 1024 * 1024

F32 = jnp.float32
BF16 = jnp.bfloat16


def _rms(x, g):
    return x * lax.rsqrt(jnp.mean(x * x, axis=-1, keepdims=True) + RMS_EPS) * g


def _dot(a, b):
    return jnp.dot(a, b, preferred_element_type=F32)


def _dot_t(a, b):
    return lax.dot_general(a, b, (((1,), (1,)), ((), ())), preferred_element_type=F32)


def _full(shape):
    return pl.BlockSpec(shape, lambda *_: (0,) * len(shape))


def _head(hd):
    return slice(hd * HEAD_PAD, (hd + 1) * HEAD_PAD)


def _ffn_kernel(x_ref, g_ref, wg_ref, wu_ref, wd_ref, o_ref):
    x = x_ref[...]
    h = _rms(x, g_ref[...]).astype(BF16)
    gate = _dot(h, wg_ref[...])
    up = _dot(h, wu_ref[...])
    a = (gate * jax.nn.sigmoid(gate) * up).astype(BF16)
    o_ref[...] = x + 0.5 * _dot(a, wd_ref[...])


def _ffn(x, g, wg, wu, wd, tm):
    n = x.shape[0]
    assert n % tm == 0
    return pl.pallas_call(
        _ffn_kernel,
        grid=(n // tm,),
        in_specs=[
            pl.BlockSpec((tm, D_MODEL), lambda i: (i, 0)),
            _full((1, D_MODEL)),
            _full((D_MODEL, D_FF)),
            _full((D_MODEL, D_FF)),
            _full((D_FF, D_MODEL)),
        ],
        out_specs=pl.BlockSpec((tm, D_MODEL), lambda i: (i, 0)),
        out_shape=jax.ShapeDtypeStruct((n, D_MODEL), F32),
        compiler_params=pltpu.CompilerParams(
            dimension_semantics=("arbitrary",), vmem_limit_bytes=VMEM_LIMIT),
        name="ffn",
    )(x, g, wg, wu, wd)


def _expand_keys(cb, kpr, wuk_ref, kgain, store):
    sspe = jnp.sum(kpr * kpr, axis=-1, keepdims=True)
    for hd in range(N_HEADS):
        kn = _dot(cb, wuk_ref[:, _head(hd)])
        ss = (jnp.sum(kn * kn, axis=-1, keepdims=True) + sspe) * (1.0 / QK_DIM)
        store(hd, ((kn + kpr) * lax.rsqrt(ss + RMS_EPS) * kgain).astype(BF16))


def _proj_kernel(x_ref, g_ref, win_ref, cw_ref, cinit_ref, qag_ref, wuqt_ref, kvag_ref, wukt_ref,
                 wuvt_ref, qgain_ref, kgain_ref, rc_ref, rs1_ref, rs2_ref, cost_ref, sint_ref,
                 bc_ref, sgc_ref, sgm_ref, qt_ref, k_ref, vt_ref, ckv_ref, kpe_ref, tail_ref, cbuf,
                 *, sub, tail_row):
    tm = x_ref.shape[0]
    half = QK_ROPE // 2

    @pl.when(pl.program_id(1) == 0)
    def _():
        cbuf[6:8, :] = cinit_ref[0]

    qgain = jnp.broadcast_to(qgain_ref[...], (HEAD_PAD, sub))
    kgain = jnp.broadcast_to(kgain_ref[...], (HEAD_PAD, sub))
    cw = cw_ref[...]

    for s0 in range(0, tm, sub):
        rows = slice(s0, s0 + sub)
        h = _rms(x_ref[rows, :], g_ref[...]).astype(BF16)

        def proj(lo, n):
            return _dot(h, win_ref[:, lo:lo + n])

        cin = proj(OFF_C, D_CONV) * proj(OFF_V, D_CONV)
        cbuf[8:8 + sub, :] = cin
        y = cw[0:1] * cbuf[6:6 + sub, :] + cw[1:2] * cbuf[7:7 + sub, :] + cw[2:3] * cin
        bc_ref[rows, :] = (proj(OFF_B, D_CONV) * y).astype(BF16)
        tail = cbuf[tail_row + 6:tail_row + 8, :]
        cbuf[6:8, :] = tail
        tail_ref[0] = tail

        sgc_ref[rows, :] = jax.nn.sigmoid(proj(OFF_GC, D_MODEL)).astype(BF16)
        sgm_ref[rows, :] = jax.nn.sigmoid(proj(OFF_GM, D_MODEL)).astype(BF16)

        qn = _rms(proj(OFF_QL, Q_LORA), qag_ref[...]).astype(BF16)
        qt = _dot_t(wuqt_ref[...], qn)
        cos_t, sin_t = cost_ref[:, rows], sint_ref[:, rows]
        for hd in range(N_HEADS):
            r0 = hd * HEAD_PAD
            x1 = qt[r0 + QK_NOPE:r0 + QK_NOPE + half]
            x2 = qt[r0 + QK_NOPE + half:r0 + QK_DIM]
            rot = jnp.concatenate([qt[r0:r0 + QK_NOPE], x1 * cos_t - x2 * sin_t, x2 * cos_t + x1 * sin_t,
                                   qt[r0 + QK_DIM:r0 + HEAD_PAD]], axis=0)
            ss = jnp.sum(rot * rot, axis=0, keepdims=True) * (1.0 / QK_DIM)
            qt_ref[0, _head(hd), rows] = (rot * lax.rsqrt(ss + RMS_EPS) * qgain).astype(BF16)

        ckv = _rms(proj(OFF_KV, KV_LORA), kvag_ref[...])
        ckv_ref[rows, :] = ckv
        cb = ckv.astype(BF16)
        kpb = proj(OFF_KPE, HEAD_PAD)
        kpr = (kpb * rc_ref[rows, :] + pltpu.roll(kpb, HEAD_PAD - half, 1) * rs1_ref[rows, :]
               + pltpu.roll(kpb, half, 1) * rs2_ref[rows, :])
        kpe_ref[rows, :] = kpr[:, QK_NOPE:QK_DIM]

        vt_ref[0, :, rows] = _dot_t(wuvt_ref[...], cb).astype(BF16)
        knt = _dot_t(wukt_ref[...], cb)
        kprt = kpr.T
        sspe = jnp.sum(kprt * kprt, axis=0, keepdims=True)
        for hd in range(N_HEADS):
            kn = knt[_head(hd)]
            ss = (jnp.sum(kn * kn, axis=0, keepdims=True) + sspe) * (1.0 / QK_DIM)
            kt = (kn + kprt) * lax.rsqrt(ss + RMS_EPS) * kgain
            k_ref[rows, _head(hd)] = kt.T.astype(BF16)


def _proj(x, cinit, rope_tok, rope_feat, rope_per_stream, w, nb, rows, tm, sub, tail_row):
    nt = rows // tm
    assert rows % tm == 0 and tm % sub == 0 and sub % 128 == 0
    n = nb * rows
    tok = lambda width: pl.BlockSpec((tm, width), lambda b, j: (b * nt + j, 0))
    feat = lambda width: pl.BlockSpec((1, width, tm), lambda b, j: (b, 0, j))
    if rope_per_stream:
        rope_t = pl.BlockSpec((tm, HEAD_PAD), lambda b, j: (b * nt + j, 0))
        rope_f = pl.BlockSpec((QK_ROPE // 2, tm), lambda b, j: (0, b * nt + j))
    else:
        rope_t = pl.BlockSpec((tm, HEAD_PAD), lambda b, j: (j, 0))
        rope_f = pl.BlockSpec((QK_ROPE // 2, tm), lambda b, j: (0, j))
    per_stream = pl.BlockSpec((1, 2, D_CONV), lambda b, j: (b, 0, 0))
    out_shapes = (
        jax.ShapeDtypeStruct((n, D_CONV), BF16),
        jax.ShapeDtypeStruct((n, D_MODEL), BF16),
        jax.ShapeDtypeStruct((n, D_MODEL), BF16),
        jax.ShapeDtypeStruct((nb, QK_W, rows), BF16),
        jax.ShapeDtypeStruct((n, QK_W), BF16),
        jax.ShapeDtypeStruct((nb, V_W, rows), BF16),
        jax.ShapeDtypeStruct((n, KV_LORA), F32),
        jax.ShapeDtypeStruct((n, QK_ROPE), F32),
        jax.ShapeDtypeStruct((nb, 2, D_CONV), F32),
    )
    return pl.pallas_call(
        functools.partial(_proj_kernel, sub=sub, tail_row=tail_row),
        grid=(nb, nt),
        in_specs=[
            tok(D_MODEL), _full((1, D_MODEL)), _full((D_MODEL, D_IN_P)), _full((CONV_WIDTH, D_CONV)),
            per_stream, _full((1, Q_LORA)), _full((QK_W, Q_LORA)), _full((1, KV_LORA)),
            _full((QK_W, KV_LORA)), _full((V_W, KV_LORA)), _full((HEAD_PAD, 1)), _full((HEAD_PAD, 1)),
            rope_t, rope_t, rope_t, rope_f, rope_f,
        ],
        out_specs=(tok(D_CONV), tok(D_MODEL), tok(D_MODEL), feat(QK_W), tok(QK_W), feat(V_W),
                   tok(KV_LORA), tok(QK_ROPE), per_stream),
        out_shape=out_shapes,
        scratch_shapes=[pltpu.VMEM((sub + 8, D_CONV), F32)],
        compiler_params=pltpu.CompilerParams(
            dimension_semantics=("arbitrary", "arbitrary"), vmem_limit_bytes=VMEM_LIMIT),
        name="proj",
    )(x, w["mix_norm"], w["w_in"], w["conv_w"], cinit, w["q_a_norm"], w["w_uq_t"], w["kv_a_norm"],
      w["w_uk_t"], w["w_uv_t"], w["qgain_col"], w["kgain_col"], *rope_tok, *rope_feat)


def _attn_prompt_kernel(qt_ref, k_ref, vt_ref, km_ref, vmt_ref, o_ref, s_ref, *, tq, tk, nh):
    seq = k_ref.shape[1]
    nq = seq // tq
    nsub = tq // tk
    units = [(hd, sub) for sub in range(nsub) for hd in range(nh)]
    kchunk = lax.broadcasted_iota(jnp.int32, (tk, tq), 0) // CHUNK
    qchunk = lax.broadcasted_iota(jnp.int32, (tk, tq), 1) // CHUNK
    diag_masks = [kchunk + sub * (tk // CHUNK) <= qchunk for sub in range(nsub)]

    def weighted(vt_blk, p):
        ones = jnp.ones((SUM_ROWS, vt_blk.shape[1]), BF16)
        return _dot(jnp.concatenate([vt_blk, ones], axis=0), p)

    def vrows(hd):
        return slice(hd * V_HEAD, (hd + 1) * V_HEAD)

    def queries(hd, q0):
        return qt_ref[0, _head(hd), pl.ds(q0, tq)]

    def scores(hd, sub, k0, qt):
        s = _dot(k_ref[0, pl.ds(k0 + sub * tk, tk), _head(hd)], qt)
        s_ref[hd, sub] = s
        return jnp.max(s, axis=0, keepdims=True)

    def probs(s, m_new):
        return jnp.exp2((s - m_new).astype(BF16))

    def qblock(qi, cmax):
        q0 = pl.multiple_of(qi * tq, tq)
        qts = [queries(hd, q0) for hd in range(nh)]

        def body(kj, carry):
            state, cmax = list(carry[0]), list(carry[1])
            k0 = pl.multiple_of(kj * tq, tq)
            for u, (hd, sub) in enumerate(units):
                m, acc = state[hd]
                m_new = jnp.maximum(m, cmax[u])
                p = probs(s_ref[hd, sub], m_new)
                cmax[u] = scores(hd, sub, pl.multiple_of(k0 + tq, tq), qts[hd])
                pv = weighted(vt_ref[0, vrows(hd), pl.ds(k0 + sub * tk, tk)], p)
                state[hd] = (m_new, jnp.exp2(m - m_new) * acc + pv)
            return tuple(state), tuple(cmax)

        empty = (jnp.full((1, tq), NEG_INF, F32), jnp.zeros((V_HEAD + SUM_ROWS, tq), F32))
        state, _ = lax.fori_loop(0, qi, body, ((empty,) * nh, cmax))
        state = list(state)

        q_next = pl.multiple_of(jnp.minimum(qi + 1, nq - 1) * tq, tq)
        cmax = [None] * len(units)
        for u, (hd, sub) in enumerate(units):
            m, acc = state[hd]
            s = jnp.where(diag_masks[sub], s_ref[hd, sub], NEG_INF)
            m_new = jnp.maximum(m, jnp.max(s, axis=0, keepdims=True))
            if sub == 0:
                s_meta = _dot(km_ref[:, _head(hd)], qts[hd])
                m_new = jnp.maximum(m_new, jnp.max(s_meta, axis=0, keepdims=True))
            p = probs(s, m_new)
            cmax[u] = scores(hd, sub, 0, queries(hd, q_next))
            pv = weighted(vt_ref[0, vrows(hd), pl.ds(q0 + sub * tk, tk)], p)
            if sub == 0:
                pv = pv + weighted(vmt_ref[vrows(hd), :], probs(s_meta, m_new))
            state[hd] = (m_new, jnp.exp2(m - m_new) * acc + pv)
        outs = [acc[:V_HEAD] / acc[V_HEAD:V_HEAD + 1] for _, acc in state]
        o_ref[0, pl.ds(q0, tq), :] = jnp.concatenate(outs, axis=0).T.astype(BF16)
        return tuple(cmax)

    cmax0 = tuple(scores(hd, sub, 0, queries(hd, 0)) for hd, sub in units)
    lax.fori_loop(0, nq, qblock, cmax0)


def _attn_prompt(qt, k, vt, k_meta, vt_meta, tq, tk, nh):
    nb, seq, _ = k.shape
    assert seq % tq == 0 and tq % tk == 0 and tk % CHUNK == 0 and N_HEADS % nh == 0
    feat = lambda width: pl.BlockSpec((1, width, seq), lambda b, p: (b, p, 0))
    tok = lambda width: pl.BlockSpec((1, seq, width), lambda b, p: (b, 0, p))
    return pl.pallas_call(
        functools.partial(_attn_prompt_kernel, tq=tq, tk=tk, nh=nh),
        grid=(nb, N_HEADS // nh),
        in_specs=[feat(nh * HEAD_PAD), tok(nh * HEAD_PAD), feat(nh * V_HEAD),
                  pl.BlockSpec((N_META, nh * HEAD_PAD), lambda b, p: (0, p)),
                  pl.BlockSpec((nh * V_HEAD, N_META), lambda b, p: (p, 0))],
        out_specs=tok(nh * V_HEAD),
        out_shape=jax.ShapeDtypeStruct((nb, seq, V_W), BF16),
        scratch_shapes=[pltpu.VMEM((nh, tq // tk, tk, tq), F32)],
        compiler_params=pltpu.CompilerParams(
            dimension_semantics=("arbitrary", "arbitrary"), vmem_limit_bytes=VMEM_LIMIT),
        name="attn_prompt",
    )(qt, k, vt, k_meta, vt_meta)


def _attn_sample_kernel(q_ref, k_ref, v_ref, ckv_ref, kpe_ref, wuk_ref, wuv_ref, kgain_ref, o_ref, kc_ref):
    rows = q_ref.shape[0]
    cb = ckv_ref[0].astype(BF16)
    is_a = lax.broadcasted_iota(jnp.int32, (rows, 2 * V_HEAD), 1) < V_HEAD

    def store_k(hd, val):
        kc_ref[hd] = val

    _expand_keys(cb, kpe_ref[0], wuk_ref, kgain_ref[...], store_k)

    for pr in range(N_HEADS // 2):
        vsl = slice(pr * 2 * V_HEAD, (pr + 1) * 2 * V_HEAD)
        v_cache = _dot(cb, wuv_ref[:, vsl]).astype(BF16)
        v_new = v_ref[:, vsl]
        outs = []
        for hh in range(2):
            hd = 2 * pr + hh
            qh = q_ref[:, _head(hd)]
            s1 = _dot_t(qh, kc_ref[hd])
            s2 = _dot_t(qh, k_ref[:, _head(hd)])
            m = jnp.maximum(jnp.max(s1, axis=-1, keepdims=True), jnp.max(s2, axis=-1, keepdims=True))
            p1 = jnp.exp2(s1 - m)
            p2 = jnp.exp2(s2 - m)
            l = jnp.sum(p1, axis=-1, keepdims=True) + jnp.sum(p2, axis=-1, keepdims=True)
            outs.append((_dot(p1.astype(BF16), v_cache) + _dot(p2.astype(BF16), v_new)) / l)
        o_ref[:, vsl] = jnp.where(is_a, outs[0], outs[1]).astype(BF16)


def _attn_sample(q, k, v, ckv_cache, kpe_cache_p, w, nb, rows):
    past = ckv_cache.shape[1]
    tok = lambda width: pl.BlockSpec((rows, width), lambda b: (b, 0))
    return pl.pallas_call(
        _attn_sample_kernel,
        grid=(nb,),
        in_specs=[tok(QK_W), tok(QK_W), tok(V_W),
                  pl.BlockSpec((1, past, KV_LORA), lambda b: (b, 0, 0)),
                  pl.BlockSpec((1, past, HEAD_PAD), lambda b: (b, 0, 0)),
                  _full((KV_LORA, QK_W)), _full((KV_LORA, V_W)), _full((1, HEAD_PAD))],
        out_specs=tok(V_W),
        out_shape=jax.ShapeDtypeStruct((nb * rows, V_W), BF16),
        scratch_shapes=[pltpu.VMEM((N_HEADS, past, HEAD_PAD), BF16)],
        compiler_params=pltpu.CompilerParams(
            dimension_semantics=("arbitrary",), vmem_limit_bytes=VMEM_LIMIT),
        name="attn_sample",
    )(q, k, v, ckv_cache, kpe_cache_p, w["w_uk"], w["w_uv"], w["kgain_row"])


def _merge_kernel(x_ref, bc_ref, at_ref, sgc_ref, sgm_ref, wco_ref, wmo_ref, woa_ref, o_ref):
    merged = (sgc_ref[...].astype(F32) * _dot(bc_ref[...], wco_ref[...])
              + sgm_ref[...].astype(F32) * _dot(at_ref[...], wmo_ref[...]))
    o_ref[...] = x_ref[...] + _dot(merged.astype(BF16), woa_ref[...])


def _merge(x, bc, attn, sgc, sgm, w, tm):
    n = x.shape[0]
    assert n % tm == 0
    tok = pl.BlockSpec((tm, D_MODEL), lambda i: (i, 0))
    sq = _full((D_MODEL, D_MODEL))
    return pl.pallas_call(
        _merge_kernel,
        grid=(n // tm,),
        in_specs=[tok, tok, tok, tok, tok, sq, sq, sq],
        out_specs=tok,
        out_shape=jax.ShapeDtypeStruct((n, D_MODEL), F32),
        compiler_params=pltpu.CompilerParams(
            dimension_semantics=("arbitrary",), vmem_limit_bytes=VMEM_LIMIT),
        name="merge",
    )(x, bc, attn, sgc, sgm, w["w_conv_out"], w["w_mla_out"], w["w_out_all"])


def _rope_tables(pos):
    half = QK_ROPE // 2
    inv_freq = ROPE_THETA ** (-jnp.arange(half, dtype=F32) / half)
    ang = pos.astype(F32)[:, None] * inv_freq[None, :]
    cos, sin = jnp.cos(ang), jnp.sin(ang)
    n = pos.shape[0]
    one = jnp.ones((n, QK_NOPE), F32)
    z16 = jnp.zeros((n, half), F32)
    z32 = jnp.zeros((n, HEAD_PAD - QK_DIM), F32)
    z64 = jnp.zeros((n, QK_NOPE), F32)
    rc = jnp.concatenate([one, cos, cos, z32], axis=1)
    rs1 = jnp.concatenate([z64, -sin, z16, z32], axis=1)
    rs2 = jnp.concatenate([z64, z16, sin, z32], axis=1)
    return (rc, rs1, rs2), (cos.T, sin.T)


def _prep_weights(mix_norm, w_in_all, conv_w, w_conv_out, q_a_norm, w_uq, kv_a_norm, w_ukv, q_norm,
                  k_norm, w_mla_out, w_out_all):
    o_ql = 3 * D_CONV
    o_kv = o_ql + Q_LORA
    o_kpe = o_kv + KV_LORA
    o_gc = o_kpe + QK_ROPE
    zeros = lambda n: jnp.zeros((D_MODEL, n), w_in_all.dtype)
    w_in = jnp.concatenate(
        [w_in_all[:, :o_ql], w_in_all[:, o_gc:], w_in_all[:, o_ql:o_kpe], zeros(QK_NOPE),
         w_in_all[:, o_kpe:o_gc], zeros(HEAD_PAD - QK_DIM)], axis=1).astype(BF16)
    pad_head = lambda a: jnp.pad(a, ((0, 0), (0, 0), (0, HEAD_PAD - a.shape[-1]))).reshape(a.shape[0], QK_W)
    w_ukv3 = w_ukv.reshape(KV_LORA, N_HEADS, QK_NOPE + V_HEAD)
    pad_gain = lambda g: jnp.pad(g, (0, HEAD_PAD - QK_DIM)).astype(F32)
    w_uk = pad_head(w_ukv3[..., :QK_NOPE]).astype(BF16)
    w_uv = w_ukv3[..., QK_NOPE:].reshape(KV_LORA, V_W).astype(BF16)
    kgain = pad_gain(k_norm)
    return {
        "mix_norm": mix_norm.reshape(1, D_MODEL),
        "w_in": w_in,
        "conv_w": conv_w,
        "q_a_norm": q_a_norm.reshape(1, Q_LORA),
        "w_uq_t": pad_head(w_uq.reshape(Q_LORA, N_HEADS, QK_DIM)).astype(BF16).T,
        "kv_a_norm": kv_a_norm.reshape(1, KV_LORA),
        "w_uk": w_uk,
        "w_uk_t": w_uk.T,
        "w_uv": w_uv,
        "w_uv_t": w_uv.T,
        "qgain_col": pad_gain(q_norm * (QK_DIM ** -0.5 * LOG2E)).reshape(HEAD_PAD, 1),
        "kgain_col": kgain.reshape(HEAD_PAD, 1),
        "kgain_row": kgain.reshape(1, HEAD_PAD),
        "w_conv_out": w_conv_out.astype(BF16),
        "w_mla_out": w_mla_out.astype(BF16),
        "w_out_all": w_out_all.astype(BF16),
    }


def kernel(x_prompt, x_sample, cache_conv, cache_kv_latent, cache_k_rope, meta_tokens, ffn1_norm, ffn1_w_gate, ffn1_w_up, ffn1_w_down, mix_norm, w_in_all, conv_w, w_conv_out, q_a_norm, w_uq, kv_a_norm, w_ukv, q_norm, k_norm, w_mla_out, w_out_all, ffn2_norm, ffn2_w_gate, ffn2_w_up, ffn2_w_down):
    depth = ffn1_norm.shape[0]
    assert depth == 1
    nb, seq, _ = x_prompt.shape
    db, dseq, _ = x_sample.shape
    past = cache_kv_latent.shape[2]
    assert dseq == 2 * N_META and seq % 512 == 0

    w = _prep_weights(mix_norm[0], w_in_all[0], conv_w[0], w_conv_out[0], q_a_norm[0], w_uq[0],
                      kv_a_norm[0], w_ukv[0], q_norm[0], k_norm[0], w_mla_out[0], w_out_all[0])
    ffn1 = (ffn1_norm[0].reshape(1, D_MODEL), ffn1_w_gate[0].astype(BF16), ffn1_w_up[0].astype(BF16),
            ffn1_w_down[0].astype(BF16))
    ffn2 = (ffn2_norm[0].reshape(1, D_MODEL), ffn2_w_gate[0].astype(BF16), ffn2_w_up[0].astype(BF16),
            ffn2_w_down[0].astype(BF16))

    n_s = db * dseq
    ns = db + 1
    xs = jnp.concatenate([x_sample.reshape(n_s, D_MODEL), jnp.zeros((dseq - N_META, D_MODEL), F32),
                          meta_tokens.astype(F32)], axis=0)
    xf = x_prompt.reshape(nb * seq, D_MODEL)

    x1s = _ffn(xs, *ffn1, tm=xs.shape[0])
    x1f = _ffn(xf, *ffn1, tm=512)

    pad_rows = lambda a: jnp.pad(a.reshape(ns, dseq, -1), ((0, 0), (0, SMALL_ROWS - dseq), (0, 0)))
    pos_s = N_META + past + jnp.arange(dseq, dtype=jnp.int32)
    pos_m = jnp.concatenate([jnp.zeros((dseq - N_META,), jnp.int32), jnp.arange(N_META, dtype=jnp.int32)])
    pos_small = jnp.pad(jnp.concatenate([jnp.tile(pos_s, db), pos_m]).reshape(ns, dseq),
                        ((0, 0), (0, SMALL_ROWS - dseq))).reshape(-1)
    rope_s = _rope_tables(pos_small)
    rope_f = _rope_tables(N_META + jnp.arange(seq, dtype=jnp.int32))

    cinit_s = jnp.concatenate([cache_conv[0].astype(F32), jnp.zeros((1, 2, D_CONV), F32)], axis=0)
    (bc_s, sgc_s, sgm_s, qt_s, k_s, vt_s, ckv_s, kpe_s, tail_s) = _proj(
        pad_rows(x1s).reshape(ns * SMALL_ROWS, D_MODEL), cinit_s, *rope_s, True, w, ns,
        SMALL_ROWS, SMALL_ROWS, SMALL_ROWS, dseq)
    cinit_f = jnp.broadcast_to(tail_s[db:db + 1], (nb, 2, D_CONV))
    (bc_f, sgc_f, sgm_f, qt_f, k_f, vt_f, ckv_f, kpe_f, tail_f) = _proj(
        x1f, cinit_f, *rope_f, False, w, nb, seq, 512, 256, 256)

    unpad = lambda a: a.reshape(ns, SMALL_ROWS, -1)[:, :dseq]
    k_s3 = unpad(k_s)
    attn_f = _attn_prompt(qt_f, k_f.reshape(nb, seq, QK_W), vt_f, k_s3[db, dseq - N_META:],
                          vt_s[db, :, dseq - N_META:dseq], tq=ATTN_TQ, tk=ATTN_TQ, nh=4)

    kpe_cache_p = jnp.pad(cache_k_rope[0].astype(F32), ((0, 0), (0, 0), (QK_NOPE, HEAD_PAD - QK_DIM)))
    q_s = qt_s[:db, :, :dseq].transpose(0, 2, 1).reshape(n_s, QK_W)
    v_s = vt_s[:db, :, :dseq].transpose(0, 2, 1).reshape(n_s, V_W)
    attn_s = _attn_sample(q_s, k_s3[:db].reshape(n_s, QK_W), v_s, cache_kv_latent[0].astype(F32),
                          kpe_cache_p, w, db, dseq)

    x2f = _merge(x1f, bc_f, attn_f.reshape(nb * seq, V_W), sgc_f, sgm_f, w, 512)
    sample_rows = lambda a: unpad(a)[:db].reshape(n_s, -1)
    x2s = _merge(x1s[:n_s], sample_rows(bc_s), attn_s, sample_rows(sgc_s), sample_rows(sgm_s), w, n_s)

    y_prompt = _ffn(x2f, *ffn2, tm=512).reshape(nb, seq, D_MODEL)
    y_sample = _ffn(x2s, *ffn2, tm=n_s).reshape(db, dseq, D_MODEL)

    ckv_s3, kpe_s3 = unpad(ckv_s), unpad(kpe_s)
    meta_rows = lambda a: jnp.broadcast_to(a[db, dseq - N_META:][None], (nb, N_META, a.shape[-1]))
    new_kv_p = jnp.concatenate([meta_rows(ckv_s3), ckv_f.reshape(nb, seq, KV_LORA)], axis=1)
    new_kpe_p = jnp.concatenate([meta_rows(kpe_s3), kpe_f.reshape(nb, seq, QK_ROPE)], axis=1)
    return (y_prompt, y_sample, tail_f[None], new_kv_p[None], new_kpe_p[None], tail_s[:db][None],
            ckv_s3[:db][None], kpe_s3[:db][None])
```

```python
import functools

import jax
import jax.numpy as jnp
from jax import lax
from jax.experimental import pallas as pl
from jax.experimental.pallas import tpu as pltpu

D_MODEL = 1024
D_FF = 2816
D_CONV = 1024
CONV_WIDTH = 3
N_HEADS = 16
QK_NOPE = 64
QK_ROPE = 32
QK_DIM = QK_NOPE + QK_ROPE
V_HEAD = 64
Q_LORA = 384
KV_LORA = 128
N_META = 16
CHUNK = 64
ROPE_THETA = 10000.0
RMS_EPS = 1e-6
NEG_INF = -1e30

HEAD_PAD = 128
QK_W = N_HEADS * HEAD_PAD
V_W = N_HEADS * V_HEAD
SMALL_ROWS = 128
SUM_ROWS = 16
LOG2E = 1.4426950408889634
ATTN_TQ = 512
OFF_B, OFF_C, OFF_V, OFF_GC, OFF_GM = 0, 1024, 2048, 3072, 4096
OFF_QL = 5120
OFF_KV = OFF_QL + Q_LORA
OFF_KPE = OFF_KV + KV_LORA
D_IN_P = OFF_KPE + HEAD_PAD

VMEM_LIMIT = 58 * 1024 * 1024

F32 = jnp.float32
BF16 = jnp.bfloat16


def _rms(x, g):
    return x * lax.rsqrt(jnp.mean(x * x, axis=-1, keepdims=True) + RMS_EPS) * g


def _dot(a, b):
    return jnp.dot(a, b, preferred_element_type=F32)


def _dot_t(a, b):
    return lax.dot_general(a, b, (((1,), (1,)), ((), ())), preferred_element_type=F32)


def _full(shape):
    return pl.BlockSpec(shape, lambda *_: (0,) * len(shape))


def _head(hd):
    return slice(hd * HEAD_PAD, (hd + 1) * HEAD_PAD)


def _ffn_kernel(x_ref, g_ref, wg_ref, wu_ref, wd_ref, o_ref):
    x = x_ref[...]
    h = _rms(x, g_ref[...]).astype(BF16)
    gate = _dot(h, wg_ref[...])
    up = _dot(h, wu_ref[...])
    a = (gate * jax.nn.sigmoid(gate) * up).astype(BF16)
    o_ref[...] = x + 0.5 * _dot(a, wd_ref[...])


def _ffn(x, g, wg, wu, wd, tm):
    n = x.shape[0]
    assert n % tm == 0
    return pl.pallas_call(
        _ffn_kernel,
        grid=(n // tm,),
        in_specs=[
            pl.BlockSpec((tm, D_MODEL), lambda i: (i, 0)),
            _full((1, D_MODEL)),
            _full((D_MODEL, D_FF)),
            _full((D_MODEL, D_FF)),
            _full((D_FF, D_MODEL)),
        ],
        out_specs=pl.BlockSpec((tm, D_MODEL), lambda i: (i, 0)),
        out_shape=jax.ShapeDtypeStruct((n, D_MODEL), F32),
        compiler_params=pltpu.CompilerParams(
            dimension_semantics=("arbitrary",), vmem_limit_bytes=VMEM_LIMIT),
        name="ffn",
    )(x, g, wg, wu, wd)


def _proj_kernel(x_ref, g_ref, win_ref, cw_ref, cinit_ref, qag_ref, wuqt_ref, kvag_ref, wukt_ref,
                 wuvt_ref, qgain_ref, kgain_ref, rc_ref, rs1_ref, rs2_ref, cost_ref, sint_ref,
                 bc_ref, sgc_ref, sgm_ref, qt_ref, k_ref, vt_ref, ckv_ref, kpe_ref, tail_ref, cbuf,
                 *, sub, tail_row):
    tm = x_ref.shape[0]
    half = QK_ROPE // 2

    @pl.when(pl.program_id(1) == 0)
    def _():
        cbuf[6:8, :] = cinit_ref[0]

    qgain = jnp.broadcast_to(qgain_ref[...], (HEAD_PAD, sub))
    kgain = jnp.broadcast_to(kgain_ref[...], (HEAD_PAD, sub))
    cw = cw_ref[...]

    for s0 in range(0, tm, sub):
        rows = slice(s0, s0 + sub)
        h = _rms(x_ref[rows, :], g_ref[...]).astype(BF16)

        def proj(lo, n):
            return _dot(h, win_ref[:, lo:lo + n])

        cin = proj(OFF_C, D_CONV) * proj(OFF_V, D_CONV)
        cbuf[8:8 + sub, :] = cin
        y = cw[0:1] * cbuf[6:6 + sub, :] + cw[1:2] * cbuf[7:7 + sub, :] + cw[2:3] * cin
        bc_ref[rows, :] = (proj(OFF_B, D_CONV) * y).astype(BF16)
        tail = cbuf[tail_row + 6:tail_row + 8, :]
        cbuf[6:8, :] = tail
        tail_ref[0] = tail

        sgc_ref[rows, :] = jax.nn.sigmoid(proj(OFF_GC, D_MODEL)).astype(BF16)
        sgm_ref[rows, :] = jax.nn.sigmoid(proj(OFF_GM, D_MODEL)).astype(BF16)

        qn = _rms(proj(OFF_QL, Q_LORA), qag_ref[...]).astype(BF16)
        qt = _dot_t(wuqt_ref[...], qn)
        cos_t, sin_t = cost_ref[:, rows], sint_ref[:, rows]
        for hd in range(N_HEADS):
            r0 = hd * HEAD_PAD
            x1 = qt[r0 + QK_NOPE:r0 + QK_NOPE + half]
            x2 = qt[r0 + QK_NOPE + half:r0 + QK_DIM]
            rot = jnp.concatenate([qt[r0:r0 + QK_NOPE], x1 * cos_t - x2 * sin_t, x2 * cos_t + x1 * sin_t,
                                   qt[r0 + QK_DIM:r0 + HEAD_PAD]], axis=0)
            ss = jnp.sum(rot * rot, axis=0, keepdims=True) * (1.0 / QK_DIM)
            qt_ref[0, _head(hd), rows] = (rot * lax.rsqrt(ss + RMS_EPS) * qgain).astype(BF16)

        ckv = _rms(proj(OFF_KV, KV_LORA), kvag_ref[...])
        ckv_ref[rows, :] = ckv
        cb = ckv.astype(BF16)
        kpb = proj(OFF_KPE, HEAD_PAD)
        kpr = (kpb * rc_ref[rows, :] + pltpu.roll(kpb, HEAD_PAD - half, 1) * rs1_ref[rows, :]
               + pltpu.roll(kpb, half, 1) * rs2_ref[rows, :])
        kpe_ref[rows, :] = kpr[:, QK_NOPE:QK_DIM]

        vt_ref[0, :, rows] = _dot_t(wuvt_ref[...], cb).astype(BF16)
        knt = _dot_t(wukt_ref[...], cb)
        kprt = kpr.T
        sspe = jnp.sum(kprt * kprt, axis=0, keepdims=True)
        for hd in range(N_HEADS):
            kn = knt[_head(hd)]
            ss = (jnp.sum(kn * kn, axis=0, keepdims=True) + sspe) * (1.0 / QK_DIM)
            kt = (kn + kprt) * lax.rsqrt(ss + RMS_EPS) * kgain
            k_ref[rows, _head(hd)] = kt.T.astype(BF16)


def _proj(x, cinit, rope_tok, rope_feat, rope_per_stream, w, nb, rows, tm, sub, tail_row):
    nt = rows // tm
    assert rows % tm == 0 and tm % sub == 0 and sub % 128 == 0
    n = nb * rows
    tok = lambda width: pl.BlockSpec((tm, width), lambda b, j: (b * nt + j, 0))
    feat = lambda width: pl.BlockSpec((1, width, tm), lambda b, j: (b, 0, j))
    if rope_per_stream:
        rope_t = pl.BlockSpec((tm, HEAD_PAD), lambda b, j: (b * nt + j, 0))
        rope_f = pl.BlockSpec((QK_ROPE // 2, tm), lambda b, j: (0, b * nt + j))
    else:
        rope_t = pl.BlockSpec((tm, HEAD_PAD), lambda b, j: (j, 0))
        rope_f = pl.BlockSpec((QK_ROPE // 2, tm), lambda b, j: (0, j))
    per_stream = pl.BlockSpec((1, 2, D_CONV), lambda b, j: (b, 0, 0))
    out_shapes = (
        jax.ShapeDtypeStruct((n, D_CONV), BF16),
        jax.ShapeDtypeStruct((n, D_MODEL), BF16),
        jax.ShapeDtypeStruct((n, D_MODEL), BF16),
        jax.ShapeDtypeStruct((nb, QK_W, rows), BF16),
        jax.ShapeDtypeStruct((n, QK_W), BF16),
        jax.ShapeDtypeStruct((nb, V_W, rows), BF16),
        jax.ShapeDtypeStruct((n, KV_LORA), F32),
        jax.ShapeDtypeStruct((n, QK_ROPE), F32),
        jax.ShapeDtypeStruct((nb, 2, D_CONV), F32),
    )
    return pl.pallas_call(
        functools.partial(_proj_kernel, sub=sub, tail_row=tail_row),
        grid=(nb, nt),
        in_specs=[
            tok(D_MODEL), _full((1, D_MODEL)), _full((D_MODEL, D_IN_P)), _full((CONV_WIDTH, D_CONV)),
            per_stream, _full((1, Q_LORA)), _full((QK_W, Q_LORA)), _full((1, KV_LORA)),
            _full((QK_W, KV_LORA)), _full((V_W, KV_LORA)), _full((HEAD_PAD, 1)), _full((HEAD_PAD, 1)),
            rope_t, rope_t, rope_t, rope_f, rope_f,
        ],
        out_specs=(tok(D_CONV), tok(D_MODEL), tok(D_MODEL), feat(QK_W), tok(QK_W), feat(V_W),
                   tok(KV_LORA), tok(QK_ROPE), per_stream),
        out_shape=out_shapes,
        scratch_shapes=[pltpu.VMEM((sub + 8, D_CONV), F32)],
        compiler_params=pltpu.CompilerParams(
            dimension_semantics=("arbitrary", "arbitrary"), vmem_limit_bytes=VMEM_LIMIT),
        name="proj",
    )(x, w["mix_norm"], w["w_in"], w["conv_w"], cinit, w["q_a_norm"], w["w_uq_t"], w["kv_a_norm"],
      w["w_uk_t"], w["w_uv_t"], w["qgain_col"], w["kgain_col"], *rope_tok, *rope_feat)


def _attn_prompt_kernel(qt_ref, k_ref, vt_ref, km_ref, vmt_ref, o_ref, s_ref, m_ref, acc_ref, cmax_ref,
                        *, tq, tk, nh):
    seq = k_ref.shape[1]
    nq = seq // tq
    nsub = tq // tk
    units = [(hd, sub) for sub in range(nsub) for hd in range(nh)]
    kchunk = lax.broadcasted_iota(jnp.int32, (tk, tq), 0) // CHUNK
    qchunk = lax.broadcasted_iota(jnp.int32, (tk, tq), 1) // CHUNK
    diag_masks = [kchunk + sub * (tk // CHUNK) <= qchunk for sub in range(nsub)]

    def weighted(vt_blk, p):
        ones = jnp.ones((SUM_ROWS, vt_blk.shape[1]), BF16)
        return _dot(jnp.concatenate([vt_blk, ones], axis=0), p)

    def vrows(hd):
        return slice(hd * V_HEAD, (hd + 1) * V_HEAD)

    def queries(hd, q0):
        return qt_ref[0, _head(hd), pl.ds(q0, tq)]

    def scores(hd, sub, k0, qt):
        s = _dot(k_ref[0, pl.ds(k0 + sub * tk, tk), _head(hd)], qt)
        s_ref[hd, sub] = s
        return jnp.max(s, axis=0, keepdims=True)

    def probs(s, m_new):
        return jnp.exp2((s - m_new).astype(BF16))

    def update(hd, m_new, pv):
        acc_ref[hd] = jnp.exp2(m_ref[hd] - m_new) * acc_ref[hd] + pv
        m_ref[hd] = m_new

    def qblock(qi, carry):
        q0 = pl.multiple_of(qi * tq, tq)
        qts = [queries(hd, q0) for hd in range(nh)]
        for hd in range(nh):
            m_ref[hd] = jnp.full((1, tq), NEG_INF, F32)
            acc_ref[hd] = jnp.zeros((V_HEAD + SUM_ROWS, tq), F32)

        def body(kj, carry):
            k0 = pl.multiple_of(kj * tq, tq)
            for u, (hd, sub) in enumerate(units):
                m_new = jnp.maximum(m_ref[hd], cmax_ref[u])
                p = probs(s_ref[hd, sub], m_new)
                cmax_ref[u] = scores(hd, sub, pl.multiple_of(k0 + tq, tq), qts[hd])
                update(hd, m_new, weighted(vt_ref[0, vrows(hd), pl.ds(k0 + sub * tk, tk)], p))
            return carry

        lax.fori_loop(0, qi, body, 0)

        q_next = pl.multiple_of(jnp.minimum(qi + 1, nq - 1) * tq, tq)
        for u, (hd, sub) in enumerate(units):
            s = jnp.where(diag_masks[sub], s_ref[hd, sub], NEG_INF)
            m_new = jnp.maximum(m_ref[hd], jnp.max(s, axis=0, keepdims=True))
            if sub == 0:
                s_meta = _dot(km_ref[:, _head(hd)], qts[hd])
                m_new = jnp.maximum(m_new, jnp.max(s_meta, axis=0, keepdims=True))
            p = probs(s, m_new)
            cmax_ref[u] = scores(hd, sub, 0, queries(hd, q_next))
            pv = weighted(vt_ref[0, vrows(hd), pl.ds(q0 + sub * tk, tk)], p)
            if sub == 0:
                pv = pv + weighted(vmt_ref[vrows(hd), :], probs(s_meta, m_new))
            update(hd, m_new, pv)
        outs = [acc_ref[hd, :V_HEAD, :] / acc_ref[hd, V_HEAD:V_HEAD + 1, :] for hd in range(nh)]
        o_ref[0, pl.ds(q0, tq), :] = jnp.concatenate(outs, axis=0).T.astype(BF16)
        return carry

    for u, (hd, sub) in enumerate(units):
        cmax_ref[u] = scores(hd, sub, 0, queries(hd, 0))
    lax.fori_loop(0, nq, qblock, 0)


def _attn_prompt(qt, k, vt, k_meta, vt_meta, tq, tk, nh):
    nb, seq, _ = k.shape
    assert seq % tq == 0 and tq % tk == 0 and tk % CHUNK == 0 and N_HEADS % nh == 0
    feat = lambda width: pl.BlockSpec((1, width, seq), lambda b, p: (b, p, 0))
    tok = lambda width: pl.BlockSpec((1, seq, width), lambda b, p: (b, 0, p))
    return pl.pallas_call(
        functools.partial(_attn_prompt_kernel, tq=tq, tk=tk, nh=nh),
        grid=(nb, N_HEADS // nh),
        in_specs=[feat(nh * HEAD_PAD), tok(nh * HEAD_PAD), feat(nh * V_HEAD),
                  pl.BlockSpec((N_META, nh * HEAD_PAD), lambda b, p: (0, p)),
                  pl.BlockSpec((nh * V_HEAD, N_META), lambda b, p: (p, 0))],
        out_specs=tok(nh * V_HEAD),
        out_shape=jax.ShapeDtypeStruct((nb, seq, V_W), BF16),
        scratch_shapes=[pltpu.VMEM((nh, tq // tk, tk, tq), F32),
                        pltpu.VMEM((nh, 1, tq), F32),
                        pltpu.VMEM((nh, V_HEAD + SUM_ROWS, tq), F32),
                        pltpu.VMEM((nh * (tq // tk), 1, tq), F32)],
        compiler_params=pltpu.CompilerParams(
            dimension_semantics=("arbitrary", "arbitrary"), vmem_limit_bytes=VMEM_LIMIT),
        name="attn_prompt",
    )(qt, k, vt, k_meta, vt_meta)


def _attn_sample_kernel(q_ref, k_ref, v_ref, ckv_ref, kpet_ref, wuknt_ref, wuv_ref, kgain_ref, o_ref):
    rows = q_ref.shape[0]
    past = ckv_ref.shape[1]
    cb = ckv_ref[0].astype(BF16)
    knt = _dot_t(wuknt_ref[...], cb)
    v_cache = _dot(cb, wuv_ref[...]).astype(BF16)
    kpet = kpet_ref[0]
    sspe = jnp.sum(kpet * kpet, axis=0, keepdims=True)
    kgain = jnp.broadcast_to(kgain_ref[...], (HEAD_PAD, past))
    zeros = jnp.zeros((HEAD_PAD - QK_DIM, past), F32)
    is_a = lax.broadcasted_iota(jnp.int32, (rows, 2 * V_HEAD), 1) < V_HEAD

    for pr in range(N_HEADS // 2):
        vsl = slice(pr * 2 * V_HEAD, (pr + 1) * 2 * V_HEAD)
        outs = []
        for hh in range(2):
            hd = 2 * pr + hh
            kn = knt[hd * QK_NOPE:(hd + 1) * QK_NOPE]
            ss = (jnp.sum(kn * kn, axis=0, keepdims=True) + sspe) * (1.0 / QK_DIM)
            kt = (jnp.concatenate([kn, kpet, zeros], axis=0) * lax.rsqrt(ss + RMS_EPS) * kgain).astype(BF16)
            qh = q_ref[:, _head(hd)]
            s1 = _dot(qh, kt)
            s2 = _dot_t(qh, k_ref[:, _head(hd)])
            m = jnp.maximum(jnp.max(s1, axis=-1, keepdims=True), jnp.max(s2, axis=-1, keepdims=True))
            p1 = jnp.exp2(s1 - m)
            p2 = jnp.exp2(s2 - m)
            l = jnp.sum(p1, axis=-1, keepdims=True) + jnp.sum(p2, axis=-1, keepdims=True)
            outs.append((_dot(p1.astype(BF16), v_cache[:, vsl]) + _dot(p2.astype(BF16), v_ref[:, vsl])) / l)
        o_ref[:, vsl] = jnp.where(is_a, outs[0], outs[1]).astype(BF16)


def _attn_sample(q, k, v, ckv_cache, kpet_cache, w, nb, rows):
    past = ckv_cache.shape[1]
    tok = lambda width: pl.BlockSpec((rows, width), lambda b: (b, 0))
    return pl.pallas_call(
        _attn_sample_kernel,
        grid=(nb,),
        in_specs=[tok(QK_W), tok(QK_W), tok(V_W),
                  pl.BlockSpec((1, past, KV_LORA), lambda b: (b, 0, 0)),
                  pl.BlockSpec((1, QK_ROPE, past), lambda b: (b, 0, 0)),
                  _full((N_HEADS * QK_NOPE, KV_LORA)), _full((KV_LORA, V_W)), _full((HEAD_PAD, 1))],
        out_specs=tok(V_W),
        out_shape=jax.ShapeDtypeStruct((nb * rows, V_W), BF16),
        compiler_params=pltpu.CompilerParams(
            dimension_semantics=("arbitrary",), vmem_limit_bytes=VMEM_LIMIT),
        name="attn_sample",
    )(q, k, v, ckv_cache, kpet_cache, w["w_ukn_t"], w["w_uv"], w["kgain_col"])


def _merge_kernel(x_ref, bc_ref, at_ref, sgc_ref, sgm_ref, wco_ref, wmo_ref, woa_ref, o_ref):
    merged = (sgc_ref[...].astype(F32) * _dot(bc_ref[...], wco_ref[...])
              + sgm_ref[...].astype(F32) * _dot(at_ref[...], wmo_ref[...]))
    o_ref[...] = x_ref[...] + _dot(merged.astype(BF16), woa_ref[...])


def _merge(x, bc, attn, sgc, sgm, w, tm):
    n = x.shape[0]
    assert n % tm == 0
    tok = pl.BlockSpec((tm, D_MODEL), lambda i: (i, 0))
    sq = _full((D_MODEL, D_MODEL))
    return pl.pallas_call(
        _merge_kernel,
        grid=(n // tm,),
        in_specs=[tok, tok, tok, tok, tok, sq, sq, sq],
        out_specs=tok,
        out_shape=jax.ShapeDtypeStruct((n, D_MODEL), F32),
        compiler_params=pltpu.CompilerParams(
            dimension_semantics=("arbitrary",), vmem_limit_bytes=VMEM_LIMIT),
        name="merge",
    )(x, bc, attn, sgc, sgm, w["w_conv_out"], w["w_mla_out"], w["w_out_all"])


def _rope_tables(pos):
    half = QK_ROPE // 2
    inv_freq = ROPE_THETA ** (-jnp.arange(half, dtype=F32) / half)
    ang = pos.astype(F32)[:, None] * inv_freq[None, :]
    cos, sin = jnp.cos(ang), jnp.sin(ang)
    n = pos.shape[0]
    one = jnp.ones((n, QK_NOPE), F32)
    z16 = jnp.zeros((n, half), F32)
    z32 = jnp.zeros((n, HEAD_PAD - QK_DIM), F32)
    z64 = jnp.zeros((n, QK_NOPE), F32)
    rc = jnp.concatenate([one, cos, cos, z32], axis=1)
    rs1 = jnp.concatenate([z64, -sin, z16, z32], axis=1)
    rs2 = jnp.concatenate([z64, z16, sin, z32], axis=1)
    return (rc, rs1, rs2), (cos.T, sin.T)


def _prep_weights(mix_norm, w_in_all, conv_w, w_conv_out, q_a_norm, w_uq, kv_a_norm, w_ukv, q_norm,
                  k_norm, w_mla_out, w_out_all):
    o_ql = 3 * D_CONV
    o_kv = o_ql + Q_LORA
    o_kpe = o_kv + KV_LORA
    o_gc = o_kpe + QK_ROPE
    zeros = lambda n: jnp.zeros((D_MODEL, n), w_in_all.dtype)
    w_in = jnp.concatenate(
        [w_in_all[:, :o_ql], w_in_all[:, o_gc:], w_in_all[:, o_ql:o_kpe], zeros(QK_NOPE),
         w_in_all[:, o_kpe:o_gc], zeros(HEAD_PAD - QK_DIM)], axis=1).astype(BF16)
    pad_head = lambda a: jnp.pad(a, ((0, 0), (0, 0), (0, HEAD_PAD - a.shape[-1]))).reshape(a.shape[0], QK_W)
    w_ukv3 = w_ukv.reshape(KV_LORA, N_HEADS, QK_NOPE + V_HEAD)
    pad_gain = lambda g: jnp.pad(g, (0, HEAD_PAD - QK_DIM)).astype(F32)
    w_ukn = w_ukv3[..., :QK_NOPE]
    w_uv = w_ukv3[..., QK_NOPE:].reshape(KV_LORA, V_W).astype(BF16)
    return {
        "mix_norm": mix_norm.reshape(1, D_MODEL),
        "w_in": w_in,
        "conv_w": conv_w,
        "q_a_norm": q_a_norm.reshape(1, Q_LORA),
        "w_uq_t": pad_head(w_uq.reshape(Q_LORA, N_HEADS, QK_DIM)).astype(BF16).T,
        "kv_a_norm": kv_a_norm.reshape(1, KV_LORA),
        "w_uk_t": pad_head(w_ukn).astype(BF16).T,
        "w_ukn_t": w_ukn.reshape(KV_LORA, N_HEADS * QK_NOPE).astype(BF16).T,
        "w_uv": w_uv,
        "w_uv_t": w_uv.T,
        "qgain_col": pad_gain(q_norm * (QK_DIM ** -0.5 * LOG2E)).reshape(HEAD_PAD, 1),
        "kgain_col": pad_gain(k_norm).reshape(HEAD_PAD, 1),
        "w_conv_out": w_conv_out.astype(BF16),
        "w_mla_out": w_mla_out.astype(BF16),
        "w_out_all": w_out_all.astype(BF16),
    }


def kernel(x_prompt, x_sample, cache_conv, cache_kv_latent, cache_k_rope, meta_tokens, ffn1_norm, ffn1_w_gate, ffn1_w_up, ffn1_w_down, mix_norm, w_in_all, conv_w, w_conv_out, q_a_norm, w_uq, kv_a_norm, w_ukv, q_norm, k_norm, w_mla_out, w_out_all, ffn2_norm, ffn2_w_gate, ffn2_w_up, ffn2_w_down):
    depth = ffn1_norm.shape[0]
    assert depth == 1
    nb, seq, _ = x_prompt.shape
    db, dseq, _ = x_sample.shape
    past = cache_kv_latent.shape[2]
    assert dseq == 2 * N_META and seq % 512 == 0

    w = _prep_weights(mix_norm[0], w_in_all[0], conv_w[0], w_conv_out[0], q_a_norm[0], w_uq[0],
                      kv_a_norm[0], w_ukv[0], q_norm[0], k_norm[0], w_mla_out[0], w_out_all[0])
    ffn1 = (ffn1_norm[0].reshape(1, D_MODEL), ffn1_w_gate[0].astype(BF16), ffn1_w_up[0].astype(BF16),
            ffn1_w_down[0].astype(BF16))
    ffn2 = (ffn2_norm[0].reshape(1, D_MODEL), ffn2_w_gate[0].astype(BF16), ffn2_w_up[0].astype(BF16),
            ffn2_w_down[0].astype(BF16))

    n_s = db * dseq
    ns = db + 1
    xs = jnp.concatenate([x_sample.reshape(n_s, D_MODEL), jnp.zeros((dseq - N_META, D_MODEL), F32),
                          meta_tokens.astype(F32)], axis=0)
    xf = x_prompt.reshape(nb * seq, D_MODEL)

    x1s = _ffn(xs, *ffn1, tm=xs.shape[0])
    x1f = _ffn(xf, *ffn1, tm=512)

    pad_rows = lambda a: jnp.pad(a.reshape(ns, dseq, -1), ((0, 0), (0, SMALL_ROWS - dseq), (0, 0)))
    pos_s = N_META + past + jnp.arange(dseq, dtype=jnp.int32)
    pos_m = jnp.concatenate([jnp.zeros((dseq - N_META,), jnp.int32), jnp.arange(N_META, dtype=jnp.int32)])
    pos_small = jnp.pad(jnp.concatenate([jnp.tile(pos_s, db), pos_m]).reshape(ns, dseq),
                        ((0, 0), (0, SMALL_ROWS - dseq))).reshape(-1)
    rope_s = _rope_tables(pos_small)
    rope_f = _rope_tables(N_META + jnp.arange(seq, dtype=jnp.int32))

    cinit_s = jnp.concatenate([cache_conv[0].astype(F32), jnp.zeros((1, 2, D_CONV), F32)], axis=0)
    (bc_s, sgc_s, sgm_s, qt_s, k_s, vt_s, ckv_s, kpe_s, tail_s) = _proj(
        pad_rows(x1s).reshape(ns * SMALL_ROWS, D_MODEL), cinit_s, *rope_s, True, w, ns,
        SMALL_ROWS, SMALL_ROWS, SMALL_ROWS, dseq)
    cinit_f = jnp.broadcast_to(tail_s[db:db + 1], (nb, 2, D_CONV))
    (bc_f, sgc_f, sgm_f, qt_f, k_f, vt_f, ckv_f, kpe_f, tail_f) = _proj(
        x1f, cinit_f, *rope_f, False, w, nb, seq, 512, 256, 256)

    unpad = lambda a: a.reshape(ns, SMALL_ROWS, -1)[:, :dseq]
    k_s3 = unpad(k_s)
    attn_f = _attn_prompt(qt_f, k_f.reshape(nb, seq, QK_W), vt_f, k_s3[db, dseq - N_META:],
                          vt_s[db, :, dseq - N_META:dseq], tq=ATTN_TQ, tk=ATTN_TQ, nh=4)

    kpet_cache = cache_k_rope[0].astype(F32).transpose(0, 2, 1)
    q_s = qt_s[:db, :, :dseq].transpose(0, 2, 1).reshape(n_s, QK_W)
    v_s = vt_s[:db, :, :dseq].transpose(0, 2, 1).reshape(n_s, V_W)
    attn_s = _attn_sample(q_s, k_s3[:db].reshape(n_s, QK_W), v_s, cache_kv_latent[0].astype(F32),
                          kpet_cache, w, db, dseq)

    x2f = _merge(x1f, bc_f, attn_f.reshape(nb * seq, V_W), sgc_f, sgm_f, w, 512)
    sample_rows = lambda a: unpad(a)[:db].reshape(n_s, -1)
    x2s = _merge(x1s[:n_s], sample_rows(bc_s), attn_s, sample_rows(sgc_s), sample_rows(sgm_s), w, n_s)

    y_prompt = _ffn(x2f, *ffn2, tm=512).reshape(nb, seq, D_MODEL)
    y_sample = _ffn(x2s, *ffn2, tm=n_s).reshape(db, dseq, D_MODEL)

    ckv_s3, kpe_s3 = unpad(ckv_s), unpad(kpe_s)
    meta_rows = lambda a: jnp.broadcast_to(a[db, dseq - N_META:][None], (nb, N_META, a.shape[-1]))
    new_kv_p = jnp.concatenate([meta_rows(ckv_s3), ckv_f.reshape(nb, seq, KV_LORA)], axis=1)
    new_kpe_p = jnp.concatenate([meta_rows(kpe_s3), kpe_f.reshape(nb, seq, QK_ROPE)], axis=1)
    return (y_prompt, y_sample, tail_f[None], new_kv_p[None], new_kpe_p[None], tail_s[:db][None],
            ckv_s3[:db][None], kpe_s3[:db][None])
```

```python
import functools

import jax
import jax.numpy as jnp
from jax import lax
from jax.experimental import pallas as pl
from jax.experimental.pallas import tpu as pltpu

D_MODEL = 1024
D_FF = 2816
D_CONV = 1024
CONV_WIDTH = 3
N_HEADS = 16
QK_NOPE = 64
QK_ROPE = 32
QK_DIM = QK_NOPE + QK_ROPE
V_HEAD = 64
Q_LORA = 384
KV_LORA = 128
N_META = 16
CHUNK = 64
ROPE_THETA = 10000.0
RMS_EPS = 1e-6
NEG_INF = -1e30

HEAD_PAD = 128
QK_W = N_HEADS * HEAD_PAD
V_W = N_HEADS * V_HEAD
SMALL_ROWS = 128
SUM_ROWS = 16
LOG2E = 1.4426950408889634
ATTN_TQ = 512
OFF_B, OFF_C, OFF_V, OFF_GC, OFF_GM = 0, 1024, 2048, 3072, 4096
OFF_QL = 5120
OFF_KV = OFF_QL + Q_LORA
OFF_KPE = OFF_KV + KV_LORA
D_IN_P = OFF_KPE + HEAD_PAD

VMEM_LIMIT = 58 * 1024 * 1024

F32 = jnp.float32
BF16 = jnp.bfloat16


def _rms(x, g):
    return x * lax.rsqrt(jnp.mean(x * x, axis=-1, keepdims=True) + RMS_EPS) * g


def _dot(a, b):
    return jnp.dot(a, b, preferred_element_type=F32)


def _dot_t(a, b):
    return lax.dot_general(a, b, (((1,), (1,)), ((), ())), preferred_element_type=F32)


def _full(shape):
    return pl.BlockSpec(shape, lambda *_: (0,) * len(shape))


def _head(hd):
    return slice(hd * HEAD_PAD, (hd + 1) * HEAD_PAD)


def _ffn_kernel(x_ref, g_ref, wg_ref, wu_ref, wd_ref, o_ref):
    x = x_ref[...]
    h = _rms(x, g_ref[...]).astype(BF16)
    gate = _dot(h, wg_ref[...])
    up = _dot(h, wu_ref[...])
    a = (gate * jax.nn.sigmoid(gate) * up).astype(BF16)
    o_ref[...] = x + 0.5 * _dot(a, wd_ref[...])


def _ffn(x, g, wg, wu, wd, tm):
    n = x.shape[0]
    assert n % tm == 0
    return pl.pallas_call(
        _ffn_kernel,
        grid=(n // tm,),
        in_specs=[
            pl.BlockSpec((tm, D_MODEL), lambda i: (i, 0)),
            _full((1, D_MODEL)),
            _full((D_MODEL, D_FF)),
            _full((D_MODEL, D_FF)),
            _full((D_FF, D_MODEL)),
        ],
        out_specs=pl.BlockSpec((tm, D_MODEL), lambda i: (i, 0)),
        out_shape=jax.ShapeDtypeStruct((n, D_MODEL), F32),
        compiler_params=pltpu.CompilerParams(
            dimension_semantics=("arbitrary",), vmem_limit_bytes=VMEM_LIMIT),
        name="ffn",
    )(x, g, wg, wu, wd)


def _proj_kernel(x_ref, g_ref, win_ref, cw_ref, cinit_ref, qag_ref, wuqt_ref, kvag_ref, wukt_ref,
                 wuvt_ref, qgain_ref, kgain_ref, rc_ref, rs1_ref, rs2_ref, cost_ref, sint_ref,
                 bc_ref, sgc_ref, sgm_ref, qt_ref, k_ref, vt_ref, ckv_ref, kpe_ref, tail_ref, cbuf,
                 *, sub, tail_row):
    tm = x_ref.shape[0]
    half = QK_ROPE // 2

    @pl.when(pl.program_id(1) == 0)
    def _():
        cbuf[6:8, :] = cinit_ref[0]

    qgain = jnp.broadcast_to(qgain_ref[...], (HEAD_PAD, sub))
    kgain = jnp.broadcast_to(kgain_ref[...], (HEAD_PAD, sub))
    cw = cw_ref[...]

    for s0 in range(0, tm, sub):
        rows = slice(s0, s0 + sub)
        h = _rms(x_ref[rows, :], g_ref[...]).astype(BF16)

        def proj(lo, n):
            return _dot(h, win_ref[:, lo:lo + n])

        cin = proj(OFF_C, D_CONV) * proj(OFF_V, D_CONV)
        cbuf[8:8 + sub, :] = cin
        y = cw[0:1] * cbuf[6:6 + sub, :] + cw[1:2] * cbuf[7:7 + sub, :] + cw[2:3] * cin
        bc_ref[rows, :] = (proj(OFF_B, D_CONV) * y).astype(BF16)
        tail = cbuf[tail_row + 6:tail_row + 8, :]
        cbuf[6:8, :] = tail
        tail_ref[0] = tail

        sgc_ref[rows, :] = jax.nn.sigmoid(proj(OFF_GC, D_MODEL)).astype(BF16)
        sgm_ref[rows, :] = jax.nn.sigmoid(proj(OFF_GM, D_MODEL)).astype(BF16)

        qn = _rms(proj(OFF_QL, Q_LORA), qag_ref[...]).astype(BF16)
        qt = _dot_t(wuqt_ref[...], qn)
        cos_t, sin_t = cost_ref[:, rows], sint_ref[:, rows]
        for hd in range(N_HEADS):
            r0 = hd * HEAD_PAD
            x1 = qt[r0 + QK_NOPE:r0 + QK_NOPE + half]
            x2 = qt[r0 + QK_NOPE + half:r0 + QK_DIM]
            rot = jnp.concatenate([qt[r0:r0 + QK_NOPE], x1 * cos_t - x2 * sin_t, x2 * cos_t + x1 * sin_t,
                                   qt[r0 + QK_DIM:r0 + HEAD_PAD]], axis=0)
            ss = jnp.sum(rot * rot, axis=0, keepdims=True) * (1.0 / QK_DIM)
            qt_ref[0, _head(hd), rows] = (rot * lax.rsqrt(ss + RMS_EPS) * qgain).astype(BF16)

        ckv = _rms(proj(OFF_KV, KV_LORA), kvag_ref[...])
        ckv_ref[rows, :] = ckv
        cb = ckv.astype(BF16)
        kpb = proj(OFF_KPE, HEAD_PAD)
        kpr = (kpb * rc_ref[rows, :] + pltpu.roll(kpb, HEAD_PAD - half, 1) * rs1_ref[rows, :]
               + pltpu.roll(kpb, half, 1) * rs2_ref[rows, :])
        kpe_ref[rows, :] = kpr[:, QK_NOPE:QK_DIM]

        vt_ref[0, :, rows] = _dot_t(wuvt_ref[...], cb).astype(BF16)
        knt = _dot_t(wukt_ref[...], cb)
        kprt = kpr.T
        sspe = jnp.sum(kprt * kprt, axis=0, keepdims=True)
        for hd in range(N_HEADS):
            kn = knt[_head(hd)]
            ss = (jnp.sum(kn * kn, axis=0, keepdims=True) + sspe) * (1.0 / QK_DIM)
            kt = (kn + kprt) * lax.rsqrt(ss + RMS_EPS) * kgain
            k_ref[rows, _head(hd)] = kt.T.astype(BF16)


def _proj(x, cinit, rope_tok, rope_feat, rope_per_stream, w, nb, rows, tm, sub, tail_row):
    nt = rows // tm
    assert rows % tm == 0 and tm % sub == 0 and sub % 128 == 0
    n = nb * rows
    tok = lambda width: pl.BlockSpec((tm, width), lambda b, j: (b * nt + j, 0))
    feat = lambda width: pl.BlockSpec((1, width, tm), lambda b, j: (b, 0, j))
    if rope_per_stream:
        rope_t = pl.BlockSpec((tm, HEAD_PAD), lambda b, j: (b * nt + j, 0))
        rope_f = pl.BlockSpec((QK_ROPE // 2, tm), lambda b, j: (0, b * nt + j))
    else:
        rope_t = pl.BlockSpec((tm, HEAD_PAD), lambda b, j: (j, 0))
        rope_f = pl.BlockSpec((QK_ROPE // 2, tm), lambda b, j: (0, j))
    per_stream = pl.BlockSpec((1, 2, D_CONV), lambda b, j: (b, 0, 0))
    out_shapes = (
        jax.ShapeDtypeStruct((n, D_CONV), BF16),
        jax.ShapeDtypeStruct((n, D_MODEL), BF16),
        jax.ShapeDtypeStruct((n, D_MODEL), BF16),
        jax.ShapeDtypeStruct((nb, QK_W, rows), BF16),
        jax.ShapeDtypeStruct((n, QK_W), BF16),
        jax.ShapeDtypeStruct((nb, V_W, rows), BF16),
        jax.ShapeDtypeStruct((n, KV_LORA), F32),
        jax.ShapeDtypeStruct((n, QK_ROPE), F32),
        jax.ShapeDtypeStruct((nb, 2, D_CONV), F32),
    )
    return pl.pallas_call(
        functools.partial(_proj_kernel, sub=sub, tail_row=tail_row),
        grid=(nb, nt),
        in_specs=[
            tok(D_MODEL), _full((1, D_MODEL)), _full((D_MODEL, D_IN_P)), _full((CONV_WIDTH, D_CONV)),
            per_stream, _full((1, Q_LORA)), _full((QK_W, Q_LORA)), _full((1, KV_LORA)),
            _full((QK_W, KV_LORA)), _full((V_W, KV_LORA)), _full((HEAD_PAD, 1)), _full((HEAD_PAD, 1)),
            rope_t, rope_t, rope_t, rope_f, rope_f,
        ],
        out_specs=(tok(D_CONV), tok(D_MODEL), tok(D_MODEL), feat(QK_W), tok(QK_W), feat(V_W),
                   tok(KV_LORA), tok(QK_ROPE), per_stream),
        out_shape=out_shapes,
        scratch_shapes=[pltpu.VMEM((sub + 8, D_CONV), F32)],
        compiler_params=pltpu.CompilerParams(
            dimension_semantics=("arbitrary", "arbitrary"), vmem_limit_bytes=VMEM_LIMIT),
        name="proj",
    )(x, w["mix_norm"], w["w_in"], w["conv_w"], cinit, w["q_a_norm"], w["w_uq_t"], w["kv_a_norm"],
      w["w_uk_t"], w["w_uv_t"], w["qgain_col"], w["kgain_col"], *rope_tok, *rope_feat)


def _attn_prompt_kernel(qt_ref, k_ref, vt_ref, km_ref, vmt_ref, o_ref, s_ref, m_ref, acc_ref, cmax_ref,
                        *, tq, tk, nh):
    seq = k_ref.shape[1]
    nq = seq // tq
    nsub = tq // tk
    units = [(hd, sub) for sub in range(nsub) for hd in range(nh)]
    kchunk = lax.broadcasted_iota(jnp.int32, (tk, tq), 0) // CHUNK
    qchunk = lax.broadcasted_iota(jnp.int32, (tk, tq), 1) // CHUNK
    diag_masks = [kchunk + sub * (tk // CHUNK) <= qchunk for sub in range(nsub)]

    def weighted(vt_blk, p):
        ones = jnp.ones((SUM_ROWS, vt_blk.shape[1]), BF16)
        return _dot(jnp.concatenate([vt_blk, ones], axis=0), p)

    def vrows(hd):
        return slice(hd * V_HEAD, (hd + 1) * V_HEAD)

    def queries(hd, q0):
        return qt_ref[0, _head(hd), pl.ds(q0, tq)]

    def scores(hd, sub, k0, qt):
        s = _dot(k_ref[0, pl.ds(k0 + sub * tk, tk), _head(hd)], qt)
        s_ref[hd, sub] = s
        return jnp.max(s, axis=0, keepdims=True)

    def probs(s, m_new):
        return jnp.exp2((s - m_new).astype(BF16))

    def update(hd, m_new, pv):
        acc_ref[hd] = jnp.exp2(m_ref[hd] - m_new) * acc_ref[hd] + pv
        m_ref[hd] = m_new

    def qblock(qi, carry):
        q0 = pl.multiple_of(qi * tq, tq)
        qts = [queries(hd, q0) for hd in range(nh)]
        for hd in range(nh):
            m_ref[hd] = jnp.full((1, tq), NEG_INF, F32)
            acc_ref[hd] = jnp.zeros((V_HEAD + SUM_ROWS, tq), F32)

        def body(kj, carry):
            k0 = pl.multiple_of(kj * tq, tq)
            for u, (hd, sub) in enumerate(units):
                m_new = jnp.maximum(m_ref[hd], cmax_ref[u])
                p = probs(s_ref[hd, sub], m_new)
                cmax_ref[u] = scores(hd, sub, pl.multiple_of(k0 + tq, tq), qts[hd])
                update(hd, m_new, weighted(vt_ref[0, vrows(hd), pl.ds(k0 + sub * tk, tk)], p))
            return carry

        lax.fori_loop(0, qi, body, 0)

        q_next = pl.multiple_of(jnp.minimum(qi + 1, nq - 1) * tq, tq)
        for u, (hd, sub) in enumerate(units):
            s = jnp.where(diag_masks[sub], s_ref[hd, sub], NEG_INF)
            m_new = jnp.maximum(m_ref[hd], jnp.max(s, axis=0, keepdims=True))
            if sub == 0:
                s_meta = _dot(km_ref[:, _head(hd)], qts[hd])
                m_new = jnp.maximum(m_new, jnp.max(s_meta, axis=0, keepdims=True))
            p = probs(s, m_new)
            cmax_ref[u] = scores(hd, sub, 0, queries(hd, q_next))
            pv = weighted(vt_ref[0, vrows(hd), pl.ds(q0 + sub * tk, tk)], p)
            if sub == 0:
                pv = pv + weighted(vmt_ref[vrows(hd), :], probs(s_meta, m_new))
            update(hd, m_new, pv)
        outs = [acc_ref[hd, :V_HEAD, :] / acc_ref[hd, V_HEAD:V_HEAD + 1, :] for hd in range(nh)]
        o_ref[0, pl.ds(q0, tq), :] = jnp.concatenate(outs, axis=0).T.astype(BF16)
        return carry

    for u, (hd, sub) in enumerate(units):
        cmax_ref[u] = scores(hd, sub, 0, queries(hd, 0))
    lax.fori_loop(0, nq, qblock, 0)


def _attn_prompt(qt, k, vt, k_meta, vt_meta, tq, tk, nh):
    nb, seq, _ = k.shape
    assert seq % tq == 0 and tq % tk == 0 and tk % CHUNK == 0 and N_HEADS % nh == 0
    feat = lambda width, **kw: pl.BlockSpec((1, width, seq), lambda b, p: (b, p, 0), **kw)
    tok = lambda width: pl.BlockSpec((1, seq, width), lambda b, p: (b, 0, p))
    return pl.pallas_call(
        functools.partial(_attn_prompt_kernel, tq=tq, tk=tk, nh=nh),
        grid=(nb, N_HEADS // nh),
        in_specs=[feat(nh * HEAD_PAD, pipeline_mode=pl.Buffered(1)), tok(nh * HEAD_PAD), feat(nh * V_HEAD),
                  pl.BlockSpec((N_META, nh * HEAD_PAD), lambda b, p: (0, p)),
                  pl.BlockSpec((nh * V_HEAD, N_META), lambda b, p: (p, 0))],
        out_specs=tok(nh * V_HEAD),
        out_shape=jax.ShapeDtypeStruct((nb, seq, V_W), BF16),
        scratch_shapes=[pltpu.VMEM((nh, tq // tk, tk, tq), F32),
                        pltpu.VMEM((nh, 1, tq), F32),
                        pltpu.VMEM((nh, V_HEAD + SUM_ROWS, tq), F32),
                        pltpu.VMEM((nh * (tq // tk), 1, tq), F32)],
        compiler_params=pltpu.CompilerParams(
            dimension_semantics=("arbitrary", "arbitrary"), vmem_limit_bytes=VMEM_LIMIT),
        name="attn_prompt",
    )(qt, k, vt, k_meta, vt_meta)


def _attn_sample_kernel(q_ref, k_ref, v_ref, ckv_ref, kpet_ref, wuknt_ref, wuv_ref, kgain_ref, o_ref):
    rows = q_ref.shape[0]
    past = ckv_ref.shape[1]
    cb = ckv_ref[0].astype(BF16)
    knt = _dot_t(wuknt_ref[...], cb)
    v_cache = _dot(cb, wuv_ref[...]).astype(BF16)
    kpet = kpet_ref[0]
    sspe = jnp.sum(kpet * kpet, axis=0, keepdims=True)
    kgain = jnp.broadcast_to(kgain_ref[...], (HEAD_PAD, past))
    zeros = jnp.zeros((HEAD_PAD - QK_DIM, past), F32)
    is_a = lax.broadcasted_iota(jnp.int32, (rows, 2 * V_HEAD), 1) < V_HEAD

    for pr in range(N_HEADS // 2):
        vsl = slice(pr * 2 * V_HEAD, (pr + 1) * 2 * V_HEAD)
        outs = []
        for hh in range(2):
            hd = 2 * pr + hh
            kn = knt[hd * QK_NOPE:(hd + 1) * QK_NOPE]
            ss = (jnp.sum(kn * kn, axis=0, keepdims=True) + sspe) * (1.0 / QK_DIM)
            kt = (jnp.concatenate([kn, kpet, zeros], axis=0) * lax.rsqrt(ss + RMS_EPS) * kgain).astype(BF16)
            qh = q_ref[:, _head(hd)]
            s1 = _dot(qh, kt)
            s2 = _dot_t(qh, k_ref[:, _head(hd)])
            m = jnp.maximum(jnp.max(s1, axis=-1, keepdims=True), jnp.max(s2, axis=-1, keepdims=True))
            p1 = jnp.exp2(s1 - m)
            p2 = jnp.exp2(s2 - m)
            l = jnp.sum(p1, axis=-1, keepdims=True) + jnp.sum(p2, axis=-1, keepdims=True)
            outs.append((_dot(p1.astype(BF16), v_cache[:, vsl]) + _dot(p2.astype(BF16), v_ref[:, vsl])) / l)
        o_ref[:, vsl] = jnp.where(is_a, outs[0], outs[1]).astype(BF16)


def _attn_sample(q, k, v, ckv_cache, kpet_cache, w, nb, rows):
    past = ckv_cache.shape[1]
    tok = lambda width: pl.BlockSpec((rows, width), lambda b: (b, 0))
    return pl.pallas_call(
        _attn_sample_kernel,
        grid=(nb,),
        in_specs=[tok(QK_W), tok(QK_W), tok(V_W),
                  pl.BlockSpec((1, past, KV_LORA), lambda b: (b, 0, 0)),
                  pl.BlockSpec((1, QK_ROPE, past), lambda b: (b, 0, 0)),
                  _full((N_HEADS * QK_NOPE, KV_LORA)), _full((KV_LORA, V_W)), _full((HEAD_PAD, 1))],
        out_specs=tok(V_W),
        out_shape=jax.ShapeDtypeStruct((nb * rows, V_W), BF16),
        compiler_params=pltpu.CompilerParams(
            dimension_semantics=("arbitrary",), vmem_limit_bytes=VMEM_LIMIT),
        name="attn_sample",
    )(q, k, v, ckv_cache, kpet_cache, w["w_ukn_t"], w["w_uv"], w["kgain_col"])


def _merge_kernel(x_ref, bc_ref, at_ref, sgc_ref, sgm_ref, wco_ref, wmo_ref, woa_ref, o_ref):
    merged = (sgc_ref[...].astype(F32) * _dot(bc_ref[...], wco_ref[...])
              + sgm_ref[...].astype(F32) * _dot(at_ref[...], wmo_ref[...]))
    o_ref[...] = x_ref[...] + _dot(merged.astype(BF16), woa_ref[...])


def _merge(x, bc, attn, sgc, sgm, w, tm):
    n = x.shape[0]
    assert n % tm == 0
    tok = pl.BlockSpec((tm, D_MODEL), lambda i: (i, 0))
    sq = _full((D_MODEL, D_MODEL))
    return pl.pallas_call(
        _merge_kernel,
        grid=(n // tm,),
        in_specs=[tok, tok, tok, tok, tok, sq, sq, sq],
        out_specs=tok,
        out_shape=jax.ShapeDtypeStruct((n, D_MODEL), F32),
        compiler_params=pltpu.CompilerParams(
            dimension_semantics=("arbitrary",), vmem_limit_bytes=VMEM_LIMIT),
        name="merge",
    )(x, bc, attn, sgc, sgm, w["w_conv_out"], w["w_mla_out"], w["w_out_all"])


def _rope_tables(pos):
    half = QK_ROPE // 2
    inv_freq = ROPE_THETA ** (-jnp.arange(half, dtype=F32) / half)
    ang = pos.astype(F32)[:, None] * inv_freq[None, :]
    cos, sin = jnp.cos(ang), jnp.sin(ang)
    n = pos.shape[0]
    one = jnp.ones((n, QK_NOPE), F32)
    z16 = jnp.zeros((n, half), F32)
    z32 = jnp.zeros((n, HEAD_PAD - QK_DIM), F32)
    z64 = jnp.zeros((n, QK_NOPE), F32)
    rc = jnp.concatenate([one, cos, cos, z32], axis=1)
    rs1 = jnp.concatenate([z64, -sin, z16, z32], axis=1)
    rs2 = jnp.concatenate([z64, z16, sin, z32], axis=1)
    return (rc, rs1, rs2), (cos.T, sin.T)


def _prep_weights(mix_norm, w_in_all, conv_w, w_conv_out, q_a_norm, w_uq, kv_a_norm, w_ukv, q_norm,
                  k_norm, w_mla_out, w_out_all):
    o_ql = 3 * D_CONV
    o_kv = o_ql + Q_LORA
    o_kpe = o_kv + KV_LORA
    o_gc = o_kpe + QK_ROPE
    zeros = lambda n: jnp.zeros((D_MODEL, n), w_in_all.dtype)
    w_in = jnp.concatenate(
        [w_in_all[:, :o_ql], w_in_all[:, o_gc:], w_in_all[:, o_ql:o_kpe], zeros(QK_NOPE),
         w_in_all[:, o_kpe:o_gc], zeros(HEAD_PAD - QK_DIM)], axis=1).astype(BF16)
    pad_head = lambda a: jnp.pad(a, ((0, 0), (0, 0), (0, HEAD_PAD - a.shape[-1]))).reshape(a.shape[0], QK_W)
    w_ukv3 = w_ukv.reshape(KV_LORA, N_HEADS, QK_NOPE + V_HEAD)
    pad_gain = lambda g: jnp.pad(g, (0, HEAD_PAD - QK_DIM)).astype(F32)
    w_ukn = w_ukv3[..., :QK_NOPE]
    w_uv = w_ukv3[..., QK_NOPE:].reshape(KV_LORA, V_W).astype(BF16)
    return {
        "mix_norm": mix_norm.reshape(1, D_MODEL),
        "w_in": w_in,
        "conv_w": conv_w,
        "q_a_norm": q_a_norm.reshape(1, Q_LORA),
        "w_uq_t": pad_head(w_uq.reshape(Q_LORA, N_HEADS, QK_DIM)).astype(BF16).T,
        "kv_a_norm": kv_a_norm.reshape(1, KV_LORA),
        "w_uk_t": pad_head(w_ukn).astype(BF16).T,
        "w_ukn_t": w_ukn.reshape(KV_LORA, N_HEADS * QK_NOPE).astype(BF16).T,
        "w_uv": w_uv,
        "w_uv_t": w_uv.T,
        "qgain_col": pad_gain(q_norm * (QK_DIM ** -0.5 * LOG2E)).reshape(HEAD_PAD, 1),
        "kgain_col": pad_gain(k_norm).reshape(HEAD_PAD, 1),
        "w_conv_out": w_conv_out.astype(BF16),
        "w_mla_out": w_mla_out.astype(BF16),
        "w_out_all": w_out_all.astype(BF16),
    }


def kernel(x_prompt, x_sample, cache_conv, cache_kv_latent, cache_k_rope, meta_tokens, ffn1_norm, ffn1_w_gate, ffn1_w_up, ffn1_w_down, mix_norm, w_in_all, conv_w, w_conv_out, q_a_norm, w_uq, kv_a_norm, w_ukv, q_norm, k_norm, w_mla_out, w_out_all, ffn2_norm, ffn2_w_gate, ffn2_w_up, ffn2_w_down):
    depth = ffn1_norm.shape[0]
    assert depth == 1
    nb, seq, _ = x_prompt.shape
    db, dseq, _ = x_sample.shape
    past = cache_kv_latent.shape[2]
    assert dseq == 2 * N_META and seq % 512 == 0

    w = _prep_weights(mix_norm[0], w_in_all[0], conv_w[0], w_conv_out[0], q_a_norm[0], w_uq[0],
                      kv_a_norm[0], w_ukv[0], q_norm[0], k_norm[0], w_mla_out[0], w_out_all[0])
    ffn1 = (ffn1_norm[0].reshape(1, D_MODEL), ffn1_w_gate[0].astype(BF16), ffn1_w_up[0].astype(BF16),
            ffn1_w_down[0].astype(BF16))
    ffn2 = (ffn2_norm[0].reshape(1, D_MODEL), ffn2_w_gate[0].astype(BF16), ffn2_w_up[0].astype(BF16),
            ffn2_w_down[0].astype(BF16))

    n_s = db * dseq
    ns = db + 1
    xs = jnp.concatenate([x_sample.reshape(n_s, D_MODEL), jnp.zeros((dseq - N_META, D_MODEL), F32),
                          meta_tokens.astype(F32)], axis=0)
    xf = x_prompt.reshape(nb * seq, D_MODEL)

    x1s = _ffn(xs, *ffn1, tm=xs.shape[0])
    x1f = _ffn(xf, *ffn1, tm=512)

    pad_rows = lambda a: jnp.pad(a.reshape(ns, dseq, -1), ((0, 0), (0, SMALL_ROWS - dseq), (0, 0)))
    pos_s = N_META + past + jnp.arange(dseq, dtype=jnp.int32)
    pos_m = jnp.concatenate([jnp.zeros((dseq - N_META,), jnp.int32), jnp.arange(N_META, dtype=jnp.int32)])
    pos_small = jnp.pad(jnp.concatenate([jnp.tile(pos_s, db), pos_m]).reshape(ns, dseq),
                        ((0, 0), (0, SMALL_ROWS - dseq))).reshape(-1)
    rope_s = _rope_tables(pos_small)
    rope_f = _rope_tables(N_META + jnp.arange(seq, dtype=jnp.int32))

    cinit_s = jnp.concatenate([cache_conv[0].astype(F32), jnp.zeros((1, 2, D_CONV), F32)], axis=0)
    (bc_s, sgc_s, sgm_s, qt_s, k_s, vt_s, ckv_s, kpe_s, tail_s) = _proj(
        pad_rows(x1s).reshape(ns * SMALL_ROWS, D_MODEL), cinit_s, *rope_s, True, w, ns,
        SMALL_ROWS, SMALL_ROWS, SMALL_ROWS, dseq)
    cinit_f = jnp.broadcast_to(tail_s[db:db + 1], (nb, 2, D_CONV))
    (bc_f, sgc_f, sgm_f, qt_f, k_f, vt_f, ckv_f, kpe_f, tail_f) = _proj(
        x1f, cinit_f, *rope_f, False, w, nb, seq, 512, 256, 256)

    unpad = lambda a: a.reshape(ns, SMALL_ROWS, -1)[:, :dseq]
    k_s3 = unpad(k_s)
    attn_f = _attn_prompt(qt_f, k_f.reshape(nb, seq, QK_W), vt_f, k_s3[db, dseq - N_META:],
                          vt_s[db, :, dseq - N_META:dseq], tq=ATTN_TQ, tk=ATTN_TQ, nh=8)

    kpet_cache = cache_k_rope[0].astype(F32).transpose(0, 2, 1)
    q_s = qt_s[:db, :, :dseq].transpose(0, 2, 1).reshape(n_s, QK_W)
    v_s = vt_s[:db, :, :dseq].transpose(0, 2, 1).reshape(n_s, V_W)
    attn_s = _attn_sample(q_s, k_s3[:db].reshape(n_s, QK_W), v_s, cache_kv_latent[0].astype(F32),
                          kpet_cache, w, db, dseq)

    x2f = _merge(x1f, bc_f, attn_f.reshape(nb * seq, V_W), sgc_f, sgm_f, w, 512)
    sample_rows = lambda a: unpad(a)[:db].reshape(n_s, -1)
    x2s = _merge(x1s[:n_s], sample_rows(bc_s), attn_s, sample_rows(sgc_s), sample_rows(sgm_s), w, n_s)

    y_prompt = _ffn(x2f, *ffn2, tm=512).reshape(nb, seq, D_MODEL)
    y_sample = _ffn(x2s, *ffn2, tm=n_s).reshape(db, dseq, D_MODEL)

    ckv_s3, kpe_s3 = unpad(ckv_s), unpad(kpe_s)
    meta_rows = lambda a: jnp.broadcast_to(a[db, dseq - N_META:][None], (nb, N_META, a.shape[-1]))
    new_kv_p = jnp.concatenate([meta_rows(ckv_s3), ckv_f.reshape(nb, seq, KV_LORA)], axis=1)
    new_kpe_p = jnp.concatenate([meta_rows(kpe_s3), kpe_f.reshape(nb, seq, QK_ROPE)], axis=1)
    return (y_prompt, y_sample, tail_f[None], new_kv_p[None], new_kpe_p[None], tail_s[:db][None],
            ckv_s3[:db][None], kpe_s3[:db][None])
```

```python
import functools

import jax
import jax.numpy as jnp
from jax import lax
from jax.experimental import pallas as pl
from jax.experimental.pallas import tpu as pltpu

D_MODEL = 1024
D_FF = 2816
D_CONV = 1024
CONV_WIDTH = 3
N_HEADS = 16
QK_NOPE = 64
QK_ROPE = 32
QK_DIM = QK_NOPE + QK_ROPE
V_HEAD = 64
Q_LORA = 384
KV_LORA = 128
N_META = 16
CHUNK = 64
ROPE_THETA = 10000.0
RMS_EPS = 1e-6
NEG_INF = -1e30

HEAD_PAD = 128
QK_W = N_HEADS * HEAD_PAD
V_W = N_HEADS * V_HEAD
SMALL_ROWS = 128
SUM_ROWS = 16
LOG2E = 1.4426950408889634
ATTN_TQ = 512
OFF_B, OFF_C, OFF_V, OFF_GC, OFF_GM = 0, 1024, 2048, 3072, 4096
OFF_QL = 5120
OFF_KV = OFF_QL + Q_LORA
OFF_KPE = OFF_KV + KV_LORA
D_IN_P = OFF_KPE + HEAD_PAD

VMEM_LIMIT = 58 * 1024 * 1024

F32 = jnp.float32
BF16 = jnp.bfloat16


def _rms(x, g):
    return x * lax.rsqrt(jnp.mean(x * x, axis=-1, keepdims=True) + RMS_EPS) * g


def _dot(a, b):
    return jnp.dot(a, b, preferred_element_type=F32)


def _dot_t(a, b):
    return lax.dot_general(a, b, (((1,), (1,)), ((), ())), preferred_element_type=F32)


def _full(shape):
    return pl.BlockSpec(shape, lambda *_: (0,) * len(shape))


def _head(hd):
    return slice(hd * HEAD_PAD, (hd + 1) * HEAD_PAD)


def _ffn_kernel(x_ref, g_ref, wg_ref, wu_ref, wd_ref, o_ref, *, sub):
    for s0 in range(0, x_ref.shape[0], sub):
        rows = slice(s0, s0 + sub)
        x = x_ref[rows, :]
        h = _rms(x, g_ref[...]).astype(BF16)
        gate = _dot(h, wg_ref[...])
        up = _dot(h, wu_ref[...])
        a = (gate * jax.nn.sigmoid(gate) * up).astype(BF16)
        o_ref[rows, :] = x + 0.5 * _dot(a, wd_ref[...])


def _ffn(x, g, wg, wu, wd, tm, sub=None):
    n = x.shape[0]
    sub = tm if sub is None else sub
    assert n % tm == 0 and tm % sub == 0
    return pl.pallas_call(
        functools.partial(_ffn_kernel, sub=sub),
        grid=(n // tm,),
        in_specs=[
            pl.BlockSpec((tm, D_MODEL), lambda i: (i, 0)),
            _full((1, D_MODEL)),
            _full((D_MODEL, D_FF)),
            _full((D_MODEL, D_FF)),
            _full((D_FF, D_MODEL)),
        ],
        out_specs=pl.BlockSpec((tm, D_MODEL), lambda i: (i, 0)),
        out_shape=jax.ShapeDtypeStruct((n, D_MODEL), F32),
        compiler_params=pltpu.CompilerParams(
            dimension_semantics=("arbitrary",), vmem_limit_bytes=VMEM_LIMIT),
        name="ffn",
    )(x, g, wg, wu, wd)


def _proj_kernel(x_ref, g_ref, win_ref, cw_ref, cinit_ref, qag_ref, wuqt_ref, kvag_ref, wukt_ref,
                 wuvt_ref, qgain_ref, kgain_ref, rc_ref, rs1_ref, rs2_ref, cost_ref, sint_ref,
                 bc_ref, sgc_ref, sgm_ref, qt_ref, k_ref, vt_ref, ckv_ref, kpe_ref, tail_ref, cbuf,
                 *, sub, tail_row):
    tm = x_ref.shape[0]
    half = QK_ROPE // 2

    @pl.when(pl.program_id(1) == 0)
    def _():
        cbuf[6:8, :] = cinit_ref[0]

    qgain = jnp.broadcast_to(qgain_ref[...], (HEAD_PAD, sub))
    kgain = jnp.broadcast_to(kgain_ref[...], (HEAD_PAD, sub))
    cw = cw_ref[...]

    for s0 in range(0, tm, sub):
        rows = slice(s0, s0 + sub)
        h = _rms(x_ref[rows, :], g_ref[...]).astype(BF16)

        def proj(lo, n):
            return _dot(h, win_ref[:, lo:lo + n])

        cin = proj(OFF_C, D_CONV) * proj(OFF_V, D_CONV)
        cbuf[8:8 + sub, :] = cin
        y = cw[0:1] * cbuf[6:6 + sub, :] + cw[1:2] * cbuf[7:7 + sub, :] + cw[2:3] * cin
        bc_ref[rows, :] = (proj(OFF_B, D_CONV) * y).astype(BF16)
        tail = cbuf[tail_row + 6:tail_row + 8, :]
        cbuf[6:8, :] = tail
        tail_ref[0] = tail

        sgc_ref[rows, :] = jax.nn.sigmoid(proj(OFF_GC, D_MODEL)).astype(BF16)
        sgm_ref[rows, :] = jax.nn.sigmoid(proj(OFF_GM, D_MODEL)).astype(BF16)

        qn = _rms(proj(OFF_QL, Q_LORA), qag_ref[...]).astype(BF16)
        qt = _dot_t(wuqt_ref[...], qn)
        cos_t, sin_t = cost_ref[:, rows], sint_ref[:, rows]
        for hd in range(N_HEADS):
            r0 = hd * HEAD_PAD
            x1 = qt[r0 + QK_NOPE:r0 + QK_NOPE + half]
            x2 = qt[r0 + QK_NOPE + half:r0 + QK_DIM]
            rot = jnp.concatenate([qt[r0:r0 + QK_NOPE], x1 * cos_t - x2 * sin_t, x2 * cos_t + x1 * sin_t,
                                   qt[r0 + QK_DIM:r0 + HEAD_PAD]], axis=0)
            ss = jnp.sum(rot * rot, axis=0, keepdims=True) * (1.0 / QK_DIM)
            qt_ref[0, _head(hd), rows] = (rot * lax.rsqrt(ss + RMS_EPS) * qgain).astype(BF16)

        ckv = _rms(proj(OFF_KV, KV_LORA), kvag_ref[...])
        ckv_ref[rows, :] = ckv
        cb = ckv.astype(BF16)
        kpb = proj(OFF_KPE, HEAD_PAD)
        kpr = (kpb * rc_ref[rows, :] + pltpu.roll(kpb, HEAD_PAD - half, 1) * rs1_ref[rows, :]
               + pltpu.roll(kpb, half, 1) * rs2_ref[rows, :])
        kpe_ref[rows, :] = kpr[:, QK_NOPE:QK_DIM]

        vt_ref[0, :, rows] = _dot_t(wuvt_ref[...], cb).astype(BF16)
        knt = _dot_t(wukt_ref[...], cb)
        kprt = kpr.T
        sspe = jnp.sum(kprt * kprt, axis=0, keepdims=True)
        for hd in range(N_HEADS):
            kn = knt[_head(hd)]
            ss = (jnp.sum(kn * kn, axis=0, keepdims=True) + sspe) * (1.0 / QK_DIM)
            kt = (kn + kprt) * lax.rsqrt(ss + RMS_EPS) * kgain
            k_ref[rows, _head(hd)] = kt.T.astype(BF16)


def _proj(x, cinit, rope_tok, rope_feat, rope_per_stream, w, nb, rows, tm, sub, tail_row):
    nt = rows // tm
    assert rows % tm == 0 and tm % sub == 0 and sub % 128 == 0
    n = nb * rows
    tok = lambda width: pl.BlockSpec((tm, width), lambda b, j: (b * nt + j, 0))
    feat = lambda width: pl.BlockSpec((1, width, tm), lambda b, j: (b, 0, j))
    if rope_per_stream:
        rope_t = pl.BlockSpec((tm, HEAD_PAD), lambda b, j: (b * nt + j, 0))
        rope_f = pl.BlockSpec((QK_ROPE // 2, tm), lambda b, j: (0, b * nt + j))
    else:
        rope_t = pl.BlockSpec((tm, HEAD_PAD), lambda b, j: (j, 0))
        rope_f = pl.BlockSpec((QK_ROPE // 2, tm), lambda b, j: (0, j))
    per_stream = pl.BlockSpec((1, 2, D_CONV), lambda b, j: (b, 0, 0))
    out_shapes = (
        jax.ShapeDtypeStruct((n, D_CONV), BF16),
        jax.ShapeDtypeStruct((n, D_MODEL), BF16),
        jax.ShapeDtypeStruct((n, D_MODEL), BF16),
        jax.ShapeDtypeStruct((nb, QK_W, rows), BF16),
        jax.ShapeDtypeStruct((n, QK_W), BF16),
        jax.ShapeDtypeStruct((nb, V_W, rows), BF16),
        jax.ShapeDtypeStruct((n, KV_LORA), F32),
        jax.ShapeDtypeStruct((n, QK_ROPE), F32),
        jax.ShapeDtypeStruct((nb, 2, D_CONV), F32),
    )
    return pl.pallas_call(
        functools.partial(_proj_kernel, sub=sub, tail_row=tail_row),
        grid=(nb, nt),
        in_specs=[
            tok(D_MODEL), _full((1, D_MODEL)), _full((D_MODEL, D_IN_P)), _full((CONV_WIDTH, D_CONV)),
            per_stream, _full((1, Q_LORA)), _full((QK_W, Q_LORA)), _full((1, KV_LORA)),
            _full((QK_W, KV_LORA)), _full((V_W, KV_LORA)), _full((HEAD_PAD, 1)), _full((HEAD_PAD, 1)),
            rope_t, rope_t, rope_t, rope_f, rope_f,
        ],
        out_specs=(tok(D_CONV), tok(D_MODEL), tok(D_MODEL), feat(QK_W), tok(QK_W), feat(V_W),
                   tok(KV_LORA), tok(QK_ROPE), per_stream),
        out_shape=out_shapes,
        scratch_shapes=[pltpu.VMEM((sub + 8, D_CONV), F32)],
        compiler_params=pltpu.CompilerParams(
            dimension_semantics=("arbitrary", "arbitrary"), vmem_limit_bytes=VMEM_LIMIT),
        name="proj",
    )(x, w["mix_norm"], w["w_in"], w["conv_w"], cinit, w["q_a_norm"], w["w_uq_t"], w["kv_a_norm"],
      w["w_uk_t"], w["w_uv_t"], w["qgain_col"], w["kgain_col"], *rope_tok, *rope_feat)


def _attn_prompt_kernel(qt_ref, k_ref, vt_ref, km_ref, vmt_ref, o_ref, s_ref, m_ref, acc_ref, cmax_ref,
                        *, tq, tk, nh):
    seq = k_ref.shape[1]
    nq = seq // tq
    nsub = tq // tk
    units = [(hd, sub) for sub in range(nsub) for hd in range(nh)]
    kchunk = lax.broadcasted_iota(jnp.int32, (tk, tq), 0) // CHUNK
    qchunk = lax.broadcasted_iota(jnp.int32, (tk, tq), 1) // CHUNK
    diag_masks = [kchunk + sub * (tk // CHUNK) <= qchunk for sub in range(nsub)]

    def weighted(vt_blk, p):
        ones = jnp.ones((SUM_ROWS, vt_blk.shape[1]), BF16)
        return _dot(jnp.concatenate([vt_blk, ones], axis=0), p)

    def vrows(hd):
        return slice(hd * V_HEAD, (hd + 1) * V_HEAD)

    def queries(hd, q0):
        return qt_ref[0, _head(hd), pl.ds(q0, tq)]

    def scores(hd, sub, k0, qt):
        s = _dot(k_ref[0, pl.ds(k0 + sub * tk, tk), _head(hd)], qt)
        s_ref[hd, sub] = s
        return jnp.max(s, axis=0, keepdims=True)

    def probs(s, m_new):
        return jnp.exp2((s - m_new).astype(BF16))

    def update(hd, m_new, pv):
        acc_ref[hd] = jnp.exp2(m_ref[hd] - m_new) * acc_ref[hd] + pv
        m_ref[hd] = m_new

    def qblock(qi, carry):
        q0 = pl.multiple_of(qi * tq, tq)
        qts = [queries(hd, q0) for hd in range(nh)]
        for hd in range(nh):
            m_ref[hd] = jnp.full((1, tq), NEG_INF, F32)
            acc_ref[hd] = jnp.zeros((V_HEAD + SUM_ROWS, tq), F32)

        def body(kj, carry):
            k0 = pl.multiple_of(kj * tq, tq)
            for u, (hd, sub) in enumerate(units):
                m_new = jnp.maximum(m_ref[hd], cmax_ref[u])
                p = probs(s_ref[hd, sub], m_new)
                cmax_ref[u] = scores(hd, sub, pl.multiple_of(k0 + tq, tq), qts[hd])
                update(hd, m_new, weighted(vt_ref[0, vrows(hd), pl.ds(k0 + sub * tk, tk)], p))
            return carry

        lax.fori_loop(0, qi, body, 0)

        q_next = pl.multiple_of(jnp.minimum(qi + 1, nq - 1) * tq, tq)
        for u, (hd, sub) in enumerate(units):
            s = jnp.where(diag_masks[sub], s_ref[hd, sub], NEG_INF)
            m_new = jnp.maximum(m_ref[hd], jnp.max(s, axis=0, keepdims=True))
            if sub == 0:
                s_meta = _dot(km_ref[:, _head(hd)], qts[hd])
                m_new = jnp.maximum(m_new, jnp.max(s_meta, axis=0, keepdims=True))
            p = probs(s, m_new)
            cmax_ref[u] = scores(hd, sub, 0, queries(hd, q_next))
            pv = weighted(vt_ref[0, vrows(hd), pl.ds(q0 + sub * tk, tk)], p)
            if sub == 0:
                pv = pv + weighted(vmt_ref[vrows(hd), :], probs(s_meta, m_new))
            update(hd, m_new, pv)
        outs = [acc_ref[hd, :V_HEAD, :] / acc_ref[hd, V_HEAD:V_HEAD + 1, :] for hd in range(nh)]
        o_ref[0, pl.ds(q0, tq), :] = jnp.concatenate(outs, axis=0).T.astype(BF16)
        return carry

    for u, (hd, sub) in enumerate(units):
        cmax_ref[u] = scores(hd, sub, 0, queries(hd, 0))
    lax.fori_loop(0, nq, qblock, 0)


def _attn_prompt(qt, k, vt, k_meta, vt_meta, tq, tk, nh):
    nb, seq, _ = k.shape
    assert seq % tq == 0 and tq % tk == 0 and tk % CHUNK == 0 and N_HEADS % nh == 0
    feat = lambda width: pl.BlockSpec((1, width, seq), lambda b, p: (b, p, 0))
    tok = lambda width: pl.BlockSpec((1, seq, width), lambda b, p: (b, 0, p))
    return pl.pallas_call(
        functools.partial(_attn_prompt_kernel, tq=tq, tk=tk, nh=nh),
        grid=(nb, N_HEADS // nh),
        in_specs=[feat(nh * HEAD_PAD), tok(nh * HEAD_PAD), feat(nh * V_HEAD),
                  pl.BlockSpec((N_META, nh * HEAD_PAD), lambda b, p: (0, p)),
                  pl.BlockSpec((nh * V_HEAD, N_META), lambda b, p: (p, 0))],
        out_specs=tok(nh * V_HEAD),
        out_shape=jax.ShapeDtypeStruct((nb, seq, V_W), BF16),
        scratch_shapes=[pltpu.VMEM((nh, tq // tk, tk, tq), F32),
                        pltpu.VMEM((nh, 1, tq), F32),
                        pltpu.VMEM((nh, V_HEAD + SUM_ROWS, tq), F32),
                        pltpu.VMEM((nh * (tq // tk), 1, tq), F32)],
        compiler_params=pltpu.CompilerParams(
            dimension_semantics=("arbitrary", "arbitrary"), vmem_limit_bytes=VMEM_LIMIT),
        name="attn_prompt",
    )(qt, k, vt, k_meta, vt_meta)


def _attn_sample_kernel(q_ref, k_ref, v_ref, ckv_ref, kpet_ref, wuknt_ref, wuv_ref, kgain_ref, o_ref):
    rows = q_ref.shape[0]
    past = ckv_ref.shape[1]
    cb = ckv_ref[0].astype(BF16)
    knt = _dot_t(wuknt_ref[...], cb)
    v_cache = _dot(cb, wuv_ref[...]).astype(BF16)
    kpet = kpet_ref[0]
    sspe = jnp.sum(kpet * kpet, axis=0, keepdims=True)
    kgain = jnp.broadcast_to(kgain_ref[...], (HEAD_PAD, past))
    zeros = jnp.zeros((HEAD_PAD - QK_DIM, past), F32)
    is_a = lax.broadcasted_iota(jnp.int32, (rows, 2 * V_HEAD), 1) < V_HEAD

    for pr in range(N_HEADS // 2):
        vsl = slice(pr * 2 * V_HEAD, (pr + 1) * 2 * V_HEAD)
        outs = []
        for hh in range(2):
            hd = 2 * pr + hh
            kn = knt[hd * QK_NOPE:(hd + 1) * QK_NOPE]
            ss = (jnp.sum(kn * kn, axis=0, keepdims=True) + sspe) * (1.0 / QK_DIM)
            kt = (jnp.concatenate([kn, kpet, zeros], axis=0) * lax.rsqrt(ss + RMS_EPS) * kgain).astype(BF16)
            qh = q_ref[:, _head(hd)]
            s1 = _dot(qh, kt)
            s2 = _dot_t(qh, k_ref[:, _head(hd)])
            m = jnp.maximum(jnp.max(s1, axis=-1, keepdims=True), jnp.max(s2, axis=-1, keepdims=True))
            p1 = jnp.exp2(s1 - m)
            p2 = jnp.exp2(s2 - m)
            l = jnp.sum(p1, axis=-1, keepdims=True) + jnp.sum(p2, axis=-1, keepdims=True)
            outs.append((_dot(p1.astype(BF16), v_cache[:, vsl]) + _dot(p2.astype(BF16), v_ref[:, vsl])) / l)
        o_ref[:, vsl] = jnp.where(is_a, outs[0], outs[1]).astype(BF16)


def _attn_sample(q, k, v, ckv_cache, kpet_cache, w, nb, rows):
    past = ckv_cache.shape[1]
    tok = lambda width: pl.BlockSpec((rows, width), lambda b: (b, 0))
    return pl.pallas_call(
        _attn_sample_kernel,
        grid=(nb,),
        in_specs=[tok(QK_W), tok(QK_W), tok(V_W),
                  pl.BlockSpec((1, past, KV_LORA), lambda b: (b, 0, 0)),
                  pl.BlockSpec((1, QK_ROPE, past), lambda b: (b, 0, 0)),
                  _full((N_HEADS * QK_NOPE, KV_LORA)), _full((KV_LORA, V_W)), _full((HEAD_PAD, 1))],
        out_specs=tok(V_W),
        out_shape=jax.ShapeDtypeStruct((nb * rows, V_W), BF16),
        compiler_params=pltpu.CompilerParams(
            dimension_semantics=("arbitrary",), vmem_limit_bytes=VMEM_LIMIT),
        name="attn_sample",
    )(q, k, v, ckv_cache, kpet_cache, w["w_ukn_t"], w["w_uv"], w["kgain_col"])


def _merge_kernel(x_ref, bc_ref, at_ref, sgc_ref, sgm_ref, wco_ref, wmo_ref, woa_ref, o_ref):
    merged = (sgc_ref[...].astype(F32) * _dot(bc_ref[...], wco_ref[...])
              + sgm_ref[...].astype(F32) * _dot(at_ref[...], wmo_ref[...]))
    o_ref[...] = x_ref[...] + _dot(merged.astype(BF16), woa_ref[...])


def _merge(x, bc, attn, sgc, sgm, w, tm):
    n = x.shape[0]
    assert n % tm == 0
    tok = pl.BlockSpec((tm, D_MODEL), lambda i: (i, 0))
    sq = _full((D_MODEL, D_MODEL))
    return pl.pallas_call(
        _merge_kernel,
        grid=(n // tm,),
        in_specs=[tok, tok, tok, tok, tok, sq, sq, sq],
        out_specs=tok,
        out_shape=jax.ShapeDtypeStruct((n, D_MODEL), F32),
        compiler_params=pltpu.CompilerParams(
            dimension_semantics=("arbitrary",), vmem_limit_bytes=VMEM_LIMIT),
        name="merge",
    )(x, bc, attn, sgc, sgm, w["w_conv_out"], w["w_mla_out"], w["w_out_all"])


def _rope_tables(pos):
    half = QK_ROPE // 2
    inv_freq = ROPE_THETA ** (-jnp.arange(half, dtype=F32) / half)
    ang = pos.astype(F32)[:, None] * inv_freq[None, :]
    cos, sin = jnp.cos(ang), jnp.sin(ang)
    n = pos.shape[0]
    one = jnp.ones((n, QK_NOPE), F32)
    z16 = jnp.zeros((n, half), F32)
    z32 = jnp.zeros((n, HEAD_PAD - QK_DIM), F32)
    z64 = jnp.zeros((n, QK_NOPE), F32)
    rc = jnp.concatenate([one, cos, cos, z32], axis=1)
    rs1 = jnp.concatenate([z64, -sin, z16, z32], axis=1)
    rs2 = jnp.concatenate([z64, z16, sin, z32], axis=1)
    return (rc, rs1, rs2), (cos.T, sin.T)


def _prep_weights(mix_norm, w_in_all, conv_w, w_conv_out, q_a_norm, w_uq, kv_a_norm, w_ukv, q_norm,
                  k_norm, w_mla_out, w_out_all):
    o_ql = 3 * D_CONV
    o_kv = o_ql + Q_LORA
    o_kpe = o_kv + KV_LORA
    o_gc = o_kpe + QK_ROPE
    zeros = lambda n: jnp.zeros((D_MODEL, n), w_in_all.dtype)
    w_in = jnp.concatenate(
        [w_in_all[:, :o_ql], w_in_all[:, o_gc:], w_in_all[:, o_ql:o_kpe], zeros(QK_NOPE),
         w_in_all[:, o_kpe:o_gc], zeros(HEAD_PAD - QK_DIM)], axis=1).astype(BF16)
    pad_head = lambda a: jnp.pad(a, ((0, 0), (0, 0), (0, HEAD_PAD - a.shape[-1]))).reshape(a.shape[0], QK_W)
    w_ukv3 = w_ukv.reshape(KV_LORA, N_HEADS, QK_NOPE + V_HEAD)
    pad_gain = lambda g: jnp.pad(g, (0, HEAD_PAD - QK_DIM)).astype(F32)
    w_ukn = w_ukv3[..., :QK_NOPE]
    w_uv = w_ukv3[..., QK_NOPE:].reshape(KV_LORA, V_W).astype(BF16)
    return {
        "mix_norm": mix_norm.reshape(1, D_MODEL),
        "w_in": w_in,
        "conv_w": conv_w,
        "q_a_norm": q_a_norm.reshape(1, Q_LORA),
        "w_uq_t": pad_head(w_uq.reshape(Q_LORA, N_HEADS, QK_DIM)).astype(BF16).T,
        "kv_a_norm": kv_a_norm.reshape(1, KV_LORA),
        "w_uk_t": pad_head(w_ukn).astype(BF16).T,
        "w_ukn_t": w_ukn.reshape(KV_LORA, N_HEADS * QK_NOPE).astype(BF16).T,
        "w_uv": w_uv,
        "w_uv_t": w_uv.T,
        "qgain_col": pad_gain(q_norm * (QK_DIM ** -0.5 * LOG2E)).reshape(HEAD_PAD, 1),
        "kgain_col": pad_gain(k_norm).reshape(HEAD_PAD, 1),
        "w_conv_out": w_conv_out.astype(BF16),
        "w_mla_out": w_mla_out.astype(BF16),
        "w_out_all": w_out_all.astype(BF16),
    }


def kernel(x_prompt, x_sample, cache_conv, cache_kv_latent, cache_k_rope, meta_tokens, ffn1_norm, ffn1_w_gate, ffn1_w_up, ffn1_w_down, mix_norm, w_in_all, conv_w, w_conv_out, q_a_norm, w_uq, kv_a_norm, w_ukv, q_norm, k_norm, w_mla_out, w_out_all, ffn2_norm, ffn2_w_gate, ffn2_w_up, ffn2_w_down):
    depth = ffn1_norm.shape[0]
    assert depth == 1
    nb, seq, _ = x_prompt.shape
    db, dseq, _ = x_sample.shape
    past = cache_kv_latent.shape[2]
    assert dseq == 2 * N_META and seq % 512 == 0

    w = _prep_weights(mix_norm[0], w_in_all[0], conv_w[0], w_conv_out[0], q_a_norm[0], w_uq[0],
                      kv_a_norm[0], w_ukv[0], q_norm[0], k_norm[0], w_mla_out[0], w_out_all[0])
    ffn1 = (ffn1_norm[0].reshape(1, D_MODEL), ffn1_w_gate[0].astype(BF16), ffn1_w_up[0].astype(BF16),
            ffn1_w_down[0].astype(BF16))
    ffn2 = (ffn2_norm[0].reshape(1, D_MODEL), ffn2_w_gate[0].astype(BF16), ffn2_w_up[0].astype(BF16),
            ffn2_w_down[0].astype(BF16))

    n_s = db * dseq
    ns = db + 1
    xs = jnp.concatenate([x_sample.reshape(n_s, D_MODEL), jnp.zeros((dseq - N_META, D_MODEL), F32),
                          meta_tokens.astype(F32)], axis=0)
    xf = x_prompt.reshape(nb * seq, D_MODEL)

    x1s = _ffn(xs, *ffn1, tm=xs.shape[0])
    x1f = _ffn(xf, *ffn1, tm=1024, sub=512)

    pad_rows = lambda a: jnp.pad(a.reshape(ns, dseq, -1), ((0, 0), (0, SMALL_ROWS - dseq), (0, 0)))
    pos_s = N_META + past + jnp.arange(dseq, dtype=jnp.int32)
    pos_m = jnp.concatenate([jnp.zeros((dseq - N_META,), jnp.int32), jnp.arange(N_META, dtype=jnp.int32)])
    pos_small = jnp.pad(jnp.concatenate([jnp.tile(pos_s, db), pos_m]).reshape(ns, dseq),
                        ((0, 0), (0, SMALL_ROWS - dseq))).reshape(-1)
    rope_s = _rope_tables(pos_small)
    rope_f = _rope_tables(N_META + jnp.arange(seq, dtype=jnp.int32))

    cinit_s = jnp.concatenate([cache_conv[0].astype(F32), jnp.zeros((1, 2, D_CONV), F32)], axis=0)
    (bc_s, sgc_s, sgm_s, qt_s, k_s, vt_s, ckv_s, kpe_s, tail_s) = _proj(
        pad_rows(x1s).reshape(ns * SMALL_ROWS, D_MODEL), cinit_s, *rope_s, True, w, ns,
        SMALL_ROWS, SMALL_ROWS, SMALL_ROWS, dseq)
    cinit_f = jnp.broadcast_to(tail_s[db:db + 1], (nb, 2, D_CONV))
    (bc_f, sgc_f, sgm_f, qt_f, k_f, vt_f, ckv_f, kpe_f, tail_f) = _proj(
        x1f, cinit_f, *rope_f, False, w, nb, seq, 512, 256, 256)

    unpad = lambda a: a.reshape(ns, SMALL_ROWS, -1)[:, :dseq]
    k_s3 = unpad(k_s)
    attn_f = _attn_prompt(qt_f, k_f.reshape(nb, seq, QK_W), vt_f, k_s3[db, dseq - N_META:],
                          vt_s[db, :, dseq - N_META:dseq], tq=ATTN_TQ, tk=ATTN_TQ // 2, nh=4)

    kpet_cache = cache_k_rope[0].astype(F32).transpose(0, 2, 1)
    q_s = qt_s[:db, :, :dseq].transpose(0, 2, 1).reshape(n_s, QK_W)
    v_s = vt_s[:db, :, :dseq].transpose(0, 2, 1).reshape(n_s, V_W)
    attn_s = _attn_sample(q_s, k_s3[:db].reshape(n_s, QK_W), v_s, cache_kv_latent[0].astype(F32),
                          kpet_cache, w, db, dseq)

    x2f = _merge(x1f, bc_f, attn_f.reshape(nb * seq, V_W), sgc_f, sgm_f, w, 1024)
    sample_rows = lambda a: unpad(a)[:db].reshape(n_s, -1)
    x2s = _merge(x1s[:n_s], sample_rows(bc_s), attn_s, sample_rows(sgc_s), sample_rows(sgm_s), w, n_s)

    y_prompt = _ffn(x2f, *ffn2, tm=1024, sub=512).reshape(nb, seq, D_MODEL)
    y_sample = _ffn(x2s, *ffn2, tm=n_s).reshape(db, dseq, D_MODEL)

    ckv_s3, kpe_s3 = unpad(ckv_s), unpad(kpe_s)
    meta_rows = lambda a: jnp.broadcast_to(a[db, dseq - N_META:][None], (nb, N_META, a.shape[-1]))
    new_kv_p = jnp.concatenate([meta_rows(ckv_s3), ckv_f.reshape(nb, seq, KV_LORA)], axis=1)
    new_kpe_p = jnp.concatenate([meta_rows(kpe_s3), kpe_f.reshape(nb, seq, QK_ROPE)], axis=1)
    return (y_prompt, y_sample, tail_f[None], new_kv_p[None], new_kpe_p[None], tail_s[:db][None],
            ckv_s3[:db][None], kpe_s3[:db][None])
```

```python
import functools

import jax
import jax.numpy as jnp
from jax import lax
from jax.experimental import pallas as pl
from jax.experimental.pallas import tpu as pltpu

D_MODEL = 1024
D_FF = 2816
D_CONV = 1024
CONV_WIDTH = 3
N_HEADS = 16
QK_NOPE = 64
QK_ROPE = 32
QK_DIM = QK_NOPE + QK_ROPE
V_HEAD = 64
Q_LORA = 384
KV_LORA = 128
N_META = 16
CHUNK = 64
ROPE_THETA = 10000.0
RMS_EPS = 1e-6
NEG_INF = -1e30

HEAD_PAD = 128
QK_W = N_HEADS * HEAD_PAD
V_W = N_HEADS * V_HEAD
SMALL_ROWS = 128
SUM_ROWS = 16
LOG2E = 1.4426950408889634
FFN_TM, FFN_SUB = 1024, 512
PROJ_TM, PROJ_SUB = 512, 256
MERGE_TM, MERGE_SUB = 1024, 512
ATTN_TQ = 512
ATTN_TK = 256
ATTN_HEADS = 4
OFF_B, OFF_C, OFF_V, OFF_GC, OFF_GM = 0, 1024, 2048, 3072, 4096
OFF_QL = 5120
OFF_KV = OFF_QL + Q_LORA
OFF_KPE = OFF_KV + KV_LORA
D_IN_P = OFF_KPE + HEAD_PAD

V7X_VMEM_BYTES = 64 * 1024 * 1024
VMEM_LIMIT = V7X_VMEM_BYTES - 6 * 1024 * 1024

F32 = jnp.float32
BF16 = jnp.bfloat16


def _rms(x, g):
    return x * lax.rsqrt(jnp.mean(x * x, axis=-1, keepdims=True) + RMS_EPS) * g


def _dot(a, b):
    return jnp.dot(a, b, preferred_element_type=F32)


def _dot_t(a, b):
    return lax.dot_general(a, b, (((1,), (1,)), ((), ())), preferred_element_type=F32)


def _full(shape):
    return pl.BlockSpec(shape, lambda *_: (0,) * len(shape))


def _head(hd):
    return slice(hd * HEAD_PAD, (hd + 1) * HEAD_PAD)


def _ffn_kernel(x_ref, g_ref, wg_ref, wu_ref, wd_ref, o_ref, *, sub):
    for s0 in range(0, x_ref.shape[0], sub):
        rows = slice(s0, s0 + sub)
        x = x_ref[rows, :]
        h = _rms(x, g_ref[...]).astype(BF16)
        gate = _dot(h, wg_ref[...])
        up = _dot(h, wu_ref[...])
        a = (gate * jax.nn.sigmoid(gate) * up).astype(BF16)
        o_ref[rows, :] = x + 0.5 * _dot(a, wd_ref[...])


def _ffn(x, g, wg, wu, wd, tm, sub=None):
    n = x.shape[0]
    sub = tm if sub is None else sub
    assert n % tm == 0 and tm % sub == 0
    return pl.pallas_call(
        functools.partial(_ffn_kernel, sub=sub),
        grid=(n // tm,),
        in_specs=[
            pl.BlockSpec((tm, D_MODEL), lambda i: (i, 0)),
            _full((1, D_MODEL)),
            _full((D_MODEL, D_FF)),
            _full((D_MODEL, D_FF)),
            _full((D_FF, D_MODEL)),
        ],
        out_specs=pl.BlockSpec((tm, D_MODEL), lambda i: (i, 0)),
        out_shape=jax.ShapeDtypeStruct((n, D_MODEL), F32),
        compiler_params=pltpu.CompilerParams(
            dimension_semantics=("arbitrary",), vmem_limit_bytes=VMEM_LIMIT),
        name="ffn",
    )(x, g, wg, wu, wd)


def _proj_kernel(x_ref, g_ref, win_ref, cw_ref, cinit_ref, qag_ref, wuqt_ref, kvag_ref, wukt_ref,
                 wuvt_ref, qgain_ref, kgain_ref, rc_ref, rs1_ref, rs2_ref, cost_ref, sint_ref,
                 bc_ref, sgc_ref, sgm_ref, qt_ref, k_ref, vt_ref, ckv_ref, kpe_ref, tail_ref, cbuf,
                 *, sub, tail_row):
    tm = x_ref.shape[0]
    half = QK_ROPE // 2

    @pl.when(pl.program_id(1) == 0)
    def _():
        cbuf[6:8, :] = cinit_ref[0]

    qgain = jnp.broadcast_to(qgain_ref[...], (HEAD_PAD, sub))
    kgain = jnp.broadcast_to(kgain_ref[...], (HEAD_PAD, sub))
    cw = cw_ref[...]

    for s0 in range(0, tm, sub):
        rows = slice(s0, s0 + sub)
        h = _rms(x_ref[rows, :], g_ref[...]).astype(BF16)

        def proj(lo, n):
            return _dot(h, win_ref[:, lo:lo + n])

        cin = proj(OFF_C, D_CONV) * proj(OFF_V, D_CONV)
        cbuf[8:8 + sub, :] = cin
        y = cw[0:1] * cbuf[6:6 + sub, :] + cw[1:2] * cbuf[7:7 + sub, :] + cw[2:3] * cin
        bc_ref[rows, :] = (proj(OFF_B, D_CONV) * y).astype(BF16)
        tail = cbuf[tail_row + 6:tail_row + 8, :]
        cbuf[6:8, :] = tail
        tail_ref[0] = tail

        sgc_ref[rows, :] = jax.nn.sigmoid(proj(OFF_GC, D_MODEL)).astype(BF16)
        sgm_ref[rows, :] = jax.nn.sigmoid(proj(OFF_GM, D_MODEL)).astype(BF16)

        qn = _rms(proj(OFF_QL, Q_LORA), qag_ref[...]).astype(BF16)
        qt = _dot_t(wuqt_ref[...], qn)
        cos_t, sin_t = cost_ref[:, rows], sint_ref[:, rows]
        for hd in range(N_HEADS):
            r0 = hd * HEAD_PAD
            x1 = qt[r0 + QK_NOPE:r0 + QK_NOPE + half]
            x2 = qt[r0 + QK_NOPE + half:r0 + QK_DIM]
            rot = jnp.concatenate([qt[r0:r0 + QK_NOPE], x1 * cos_t - x2 * sin_t, x2 * cos_t + x1 * sin_t,
                                   qt[r0 + QK_DIM:r0 + HEAD_PAD]], axis=0)
            ss = jnp.sum(rot * rot, axis=0, keepdims=True) * (1.0 / QK_DIM)
            qt_ref[0, _head(hd), rows] = (rot * lax.rsqrt(ss + RMS_EPS) * qgain).astype(BF16)

        ckv = _rms(proj(OFF_KV, KV_LORA), kvag_ref[...])
        ckv_ref[rows, :] = ckv
        cb = ckv.astype(BF16)
        kpb = proj(OFF_KPE, HEAD_PAD)
        kpr = (kpb * rc_ref[rows, :] + pltpu.roll(kpb, HEAD_PAD - half, 1) * rs1_ref[rows, :]
               + pltpu.roll(kpb, half, 1) * rs2_ref[rows, :])
        kpe_ref[rows, :] = kpr[:, QK_NOPE:QK_DIM]

        vt_ref[0, :, rows] = _dot_t(wuvt_ref[...], cb).astype(BF16)
        knt = _dot_t(wukt_ref[...], cb)
        kprt = kpr.T
        sspe = jnp.sum(kprt * kprt, axis=0, keepdims=True)
        for hd in range(N_HEADS):
            kn = knt[_head(hd)]
            ss = (jnp.sum(kn * kn, axis=0, keepdims=True) + sspe) * (1.0 / QK_DIM)
            kt = (kn + kprt) * lax.rsqrt(ss + RMS_EPS) * kgain
            k_ref[rows, _head(hd)] = kt.T.astype(BF16)


def _proj(x, cinit, rope_tok, rope_feat, rope_per_stream, w, nb, rows, tm, sub, tail_row):
    nt = rows // tm
    assert rows % tm == 0 and tm % sub == 0 and sub % 128 == 0
    n = nb * rows
    tok = lambda width: pl.BlockSpec((tm, width), lambda b, j: (b * nt + j, 0))
    feat = lambda width: pl.BlockSpec((1, width, tm), lambda b, j: (b, 0, j))
    if rope_per_stream:
        rope_t = pl.BlockSpec((tm, HEAD_PAD), lambda b, j: (b * nt + j, 0))
        rope_f = pl.BlockSpec((QK_ROPE // 2, tm), lambda b, j: (0, b * nt + j))
    else:
        rope_t = pl.BlockSpec((tm, HEAD_PAD), lambda b, j: (j, 0))
        rope_f = pl.BlockSpec((QK_ROPE // 2, tm), lambda b, j: (0, j))
    per_stream = pl.BlockSpec((1, 2, D_CONV), lambda b, j: (b, 0, 0))
    out_shapes = (
        jax.ShapeDtypeStruct((n, D_CONV), BF16),
        jax.ShapeDtypeStruct((n, D_MODEL), BF16),
        jax.ShapeDtypeStruct((n, D_MODEL), BF16),
        jax.ShapeDtypeStruct((nb, QK_W, rows), BF16),
        jax.ShapeDtypeStruct((n, QK_W), BF16),
        jax.ShapeDtypeStruct((nb, V_W, rows), BF16),
        jax.ShapeDtypeStruct((n, KV_LORA), F32),
        jax.ShapeDtypeStruct((n, QK_ROPE), F32),
        jax.ShapeDtypeStruct((nb, 2, D_CONV), F32),
    )
    return pl.pallas_call(
        functools.partial(_proj_kernel, sub=sub, tail_row=tail_row),
        grid=(nb, nt),
        in_specs=[
            tok(D_MODEL), _full((1, D_MODEL)), _full((D_MODEL, D_IN_P)), _full((CONV_WIDTH, D_CONV)),
            per_stream, _full((1, Q_LORA)), _full((QK_W, Q_LORA)), _full((1, KV_LORA)),
            _full((QK_W, KV_LORA)), _full((V_W, KV_LORA)), _full((HEAD_PAD, 1)), _full((HEAD_PAD, 1)),
            rope_t, rope_t, rope_t, rope_f, rope_f,
        ],
        out_specs=(tok(D_CONV), tok(D_MODEL), tok(D_MODEL), feat(QK_W), tok(QK_W), feat(V_W),
                   tok(KV_LORA), tok(QK_ROPE), per_stream),
        out_shape=out_shapes,
        scratch_shapes=[pltpu.VMEM((sub + 8, D_CONV), F32)],
        compiler_params=pltpu.CompilerParams(
            dimension_semantics=("arbitrary", "arbitrary"), vmem_limit_bytes=VMEM_LIMIT),
        name="proj",
    )(x, w["mix_norm"], w["w_in"], w["conv_w"], cinit, w["q_a_norm"], w["w_uq_t"], w["kv_a_norm"],
      w["w_uk_t"], w["w_uv_t"], w["qgain_col"], w["kgain_col"], *rope_tok, *rope_feat)


def _attn_prompt_kernel(qt_ref, k_ref, vt_ref, km_ref, vmt_ref, o_ref, s_ref, m_ref, acc_ref, cmax_ref,
                        *, tq, tk, nh):
    seq = k_ref.shape[1]
    nq = seq // tq
    nsub = tq // tk
    units = [(hd, sub) for sub in range(nsub) for hd in range(nh)]
    kchunk = lax.broadcasted_iota(jnp.int32, (tk, tq), 0) // CHUNK
    qchunk = lax.broadcasted_iota(jnp.int32, (tk, tq), 1) // CHUNK
    diag_masks = [kchunk + sub * (tk // CHUNK) <= qchunk for sub in range(nsub)]

    def weighted(vt_blk, p):
        ones = jnp.ones((SUM_ROWS, vt_blk.shape[1]), BF16)
        return _dot(jnp.concatenate([vt_blk, ones], axis=0), p)

    def vrows(hd):
        return slice(hd * V_HEAD, (hd + 1) * V_HEAD)

    def queries(hd, q0):
        return qt_ref[0, _head(hd), pl.ds(q0, tq)]

    def scores(hd, sub, k0, qt):
        s = _dot(k_ref[0, pl.ds(k0 + sub * tk, tk), _head(hd)], qt)
        s_ref[hd, sub] = s
        return jnp.max(s, axis=0, keepdims=True)

    def probs(s, m_new):
        return jnp.exp2((s - m_new).astype(BF16))

    def update(hd, m_new, pv):
        acc_ref[hd] = jnp.exp2(m_ref[hd] - m_new) * acc_ref[hd] + pv
        m_ref[hd] = m_new

    def qblock(qi, carry):
        q0 = pl.multiple_of(qi * tq, tq)
        qts = [queries(hd, q0) for hd in range(nh)]
        for hd in range(nh):
            m_ref[hd] = jnp.full((1, tq), NEG_INF, F32)
            acc_ref[hd] = jnp.zeros((V_HEAD + SUM_ROWS, tq), F32)

        def body(kj, carry):
            k0 = pl.multiple_of(kj * tq, tq)
            for u, (hd, sub) in enumerate(units):
                m_new = jnp.maximum(m_ref[hd], cmax_ref[u])
                p = probs(s_ref[hd, sub], m_new)
                cmax_ref[u] = scores(hd, sub, pl.multiple_of(k0 + tq, tq), qts[hd])
                update(hd, m_new, weighted(vt_ref[0, vrows(hd), pl.ds(k0 + sub * tk, tk)], p))
            return carry

        lax.fori_loop(0, qi, body, 0)

        q_next = pl.multiple_of(jnp.minimum(qi + 1, nq - 1) * tq, tq)
        for u, (hd, sub) in enumerate(units):
            s = jnp.where(diag_masks[sub], s_ref[hd, sub], NEG_INF)
            m_new = jnp.maximum(m_ref[hd], jnp.max(s, axis=0, keepdims=True))
            if sub == 0:
                s_meta = _dot(km_ref[:, _head(hd)], qts[hd])
                m_new = jnp.maximum(m_new, jnp.max(s_meta, axis=0, keepdims=True))
            p = probs(s, m_new)
            cmax_ref[u] = scores(hd, sub, 0, queries(hd, q_next))
            pv = weighted(vt_ref[0, vrows(hd), pl.ds(q0 + sub * tk, tk)], p)
            if sub == 0:
                pv = pv + weighted(vmt_ref[vrows(hd), :], probs(s_meta, m_new))
            update(hd, m_new, pv)
        outs = [acc_ref[hd, :V_HEAD, :] / acc_ref[hd, V_HEAD:V_HEAD + 1, :] for hd in range(nh)]
        o_ref[0, pl.ds(q0, tq), :] = jnp.concatenate(outs, axis=0).T.astype(BF16)
        return carry

    for u, (hd, sub) in enumerate(units):
        cmax_ref[u] = scores(hd, sub, 0, queries(hd, 0))
    lax.fori_loop(0, nq, qblock, 0)


def _attn_prompt(qt, k, vt, k_meta, vt_meta, tq, tk, nh):
    nb, seq, _ = k.shape
    assert seq % tq == 0 and tq % tk == 0 and tk % CHUNK == 0 and N_HEADS % nh == 0
    feat = lambda width: pl.BlockSpec((1, width, seq), lambda b, p: (b, p, 0))
    tok = lambda width: pl.BlockSpec((1, seq, width), lambda b, p: (b, 0, p))
    return pl.pallas_call(
        functools.partial(_attn_prompt_kernel, tq=tq, tk=tk, nh=nh),
        grid=(nb, N_HEADS // nh),
        in_specs=[feat(nh * HEAD_PAD), tok(nh * HEAD_PAD), feat(nh * V_HEAD),
                  pl.BlockSpec((N_META, nh * HEAD_PAD), lambda b, p: (0, p)),
                  pl.BlockSpec((nh * V_HEAD, N_META), lambda b, p: (p, 0))],
        out_specs=tok(nh * V_HEAD),
        out_shape=jax.ShapeDtypeStruct((nb, seq, V_W), BF16),
        scratch_shapes=[pltpu.VMEM((nh, tq // tk, tk, tq), F32),
                        pltpu.VMEM((nh, 1, tq), F32),
                        pltpu.VMEM((nh, V_HEAD + SUM_ROWS, tq), F32),
                        pltpu.VMEM((nh * (tq // tk), 1, tq), F32)],
        compiler_params=pltpu.CompilerParams(
            dimension_semantics=("arbitrary", "arbitrary"), vmem_limit_bytes=VMEM_LIMIT),
        name="attn_prompt",
    )(qt, k, vt, k_meta, vt_meta)


def _attn_sample_kernel(q_ref, k_ref, v_ref, ckv_ref, kpet_ref, wuknt_ref, wuv_ref, kgain_ref, o_ref):
    rows = q_ref.shape[0]
    past = ckv_ref.shape[1]
    cb = ckv_ref[0].astype(BF16)
    knt = _dot_t(wuknt_ref[...], cb)
    v_cache = _dot(cb, wuv_ref[...]).astype(BF16)
    kpet = kpet_ref[0]
    sspe = jnp.sum(kpet * kpet, axis=0, keepdims=True)
    kgain = jnp.broadcast_to(kgain_ref[...], (HEAD_PAD, past))
    zeros = jnp.zeros((HEAD_PAD - QK_DIM, past), F32)
    is_a = lax.broadcasted_iota(jnp.int32, (rows, 2 * V_HEAD), 1) < V_HEAD

    for pr in range(N_HEADS // 2):
        vsl = slice(pr * 2 * V_HEAD, (pr + 1) * 2 * V_HEAD)
        outs = []
        for hh in range(2):
            hd = 2 * pr + hh
            kn = knt[hd * QK_NOPE:(hd + 1) * QK_NOPE]
            ss = (jnp.sum(kn * kn, axis=0, keepdims=True) + sspe) * (1.0 / QK_DIM)
            kt = (jnp.concatenate([kn, kpet, zeros], axis=0) * lax.rsqrt(ss + RMS_EPS) * kgain).astype(BF16)
            qh = q_ref[:, _head(hd)]
            s1 = _dot(qh, kt)
            s2 = _dot_t(qh, k_ref[:, _head(hd)])
            m = jnp.maximum(jnp.max(s1, axis=-1, keepdims=True), jnp.max(s2, axis=-1, keepdims=True))
            p1 = jnp.exp2(s1 - m)
            p2 = jnp.exp2(s2 - m)
            l = jnp.sum(p1, axis=-1, keepdims=True) + jnp.sum(p2, axis=-1, keepdims=True)
            outs.append((_dot(p1.astype(BF16), v_cache[:, vsl]) + _dot(p2.astype(BF16), v_ref[:, vsl])) / l)
        o_ref[:, vsl] = jnp.where(is_a, outs[0], outs[1]).astype(BF16)


def _attn_sample(q, k, v, ckv_cache, kpet_cache, w, nb, rows):
    past = ckv_cache.shape[1]
    tok = lambda width: pl.BlockSpec((rows, width), lambda b: (b, 0))
    return pl.pallas_call(
        _attn_sample_kernel,
        grid=(nb,),
        in_specs=[tok(QK_W), tok(QK_W), tok(V_W),
                  pl.BlockSpec((1, past, KV_LORA), lambda b: (b, 0, 0)),
                  pl.BlockSpec((1, QK_ROPE, past), lambda b: (b, 0, 0)),
                  _full((N_HEADS * QK_NOPE, KV_LORA)), _full((KV_LORA, V_W)), _full((HEAD_PAD, 1))],
        out_specs=tok(V_W),
        out_shape=jax.ShapeDtypeStruct((nb * rows, V_W), BF16),
        compiler_params=pltpu.CompilerParams(
            dimension_semantics=("arbitrary",), vmem_limit_bytes=VMEM_LIMIT),
        name="attn_sample",
    )(q, k, v, ckv_cache, kpet_cache, w["w_ukn_t"], w["w_uv"], w["kgain_col"])


def _merge_kernel(x_ref, bc_ref, at_ref, sgc_ref, sgm_ref, wco_ref, wmo_ref, woa_ref, o_ref, *, sub):
    for s0 in range(0, x_ref.shape[0], sub):
        rows = slice(s0, s0 + sub)
        merged = (sgc_ref[rows, :].astype(F32) * _dot(bc_ref[rows, :], wco_ref[...])
                  + sgm_ref[rows, :].astype(F32) * _dot(at_ref[rows, :], wmo_ref[...]))
        o_ref[rows, :] = x_ref[rows, :] + _dot(merged.astype(BF16), woa_ref[...])


def _merge(x, bc, attn, sgc, sgm, w, tm, sub):
    n = x.shape[0]
    assert n % tm == 0 and tm % sub == 0
    tok = pl.BlockSpec((tm, D_MODEL), lambda i: (i, 0))
    sq = _full((D_MODEL, D_MODEL))
    return pl.pallas_call(
        functools.partial(_merge_kernel, sub=sub),
        grid=(n // tm,),
        in_specs=[tok, tok, tok, tok, tok, sq, sq, sq],
        out_specs=tok,
        out_shape=jax.ShapeDtypeStruct((n, D_MODEL), F32),
        compiler_params=pltpu.CompilerParams(
            dimension_semantics=("arbitrary",), vmem_limit_bytes=VMEM_LIMIT),
        name="merge",
    )(x, bc, attn, sgc, sgm, w["w_conv_out"], w["w_mla_out"], w["w_out_all"])


def _rope_tables(pos):
    half = QK_ROPE // 2
    inv_freq = ROPE_THETA ** (-jnp.arange(half, dtype=F32) / half)
    ang = pos.astype(F32)[:, None] * inv_freq[None, :]
    cos, sin = jnp.cos(ang), jnp.sin(ang)
    n = pos.shape[0]
    one = jnp.ones((n, QK_NOPE), F32)
    z16 = jnp.zeros((n, half), F32)
    z32 = jnp.zeros((n, HEAD_PAD - QK_DIM), F32)
    z64 = jnp.zeros((n, QK_NOPE), F32)
    rc = jnp.concatenate([one, cos, cos, z32], axis=1)
    rs1 = jnp.concatenate([z64, -sin, z16, z32], axis=1)
    rs2 = jnp.concatenate([z64, z16, sin, z32], axis=1)
    return (rc, rs1, rs2), (cos.T, sin.T)


def _prep_weights(mix_norm, w_in_all, conv_w, w_conv_out, q_a_norm, w_uq, kv_a_norm, w_ukv, q_norm,
                  k_norm, w_mla_out, w_out_all):
    o_ql = 3 * D_CONV
    o_kv = o_ql + Q_LORA
    o_kpe = o_kv + KV_LORA
    o_gc = o_kpe + QK_ROPE
    zeros = lambda n: jnp.zeros((D_MODEL, n), w_in_all.dtype)
    w_in = jnp.concatenate(
        [w_in_all[:, :o_ql], w_in_all[:, o_gc:], w_in_all[:, o_ql:o_kpe], zeros(QK_NOPE),
         w_in_all[:, o_kpe:o_gc], zeros(HEAD_PAD - QK_DIM)], axis=1).astype(BF16)
    pad_head = lambda a: jnp.pad(a, ((0, 0), (0, 0), (0, HEAD_PAD - a.shape[-1]))).reshape(a.shape[0], QK_W)
    w_ukv3 = w_ukv.reshape(KV_LORA, N_HEADS, QK_NOPE + V_HEAD)
    pad_gain = lambda g: jnp.pad(g, (0, HEAD_PAD - QK_DIM)).astype(F32)
    w_ukn = w_ukv3[..., :QK_NOPE]
    w_uv = w_ukv3[..., QK_NOPE:].reshape(KV_LORA, V_W).astype(BF16)
    return {
        "mix_norm": mix_norm.reshape(1, D_MODEL),
        "w_in": w_in,
        "conv_w": conv_w,
        "q_a_norm": q_a_norm.reshape(1, Q_LORA),
        "w_uq_t": pad_head(w_uq.reshape(Q_LORA, N_HEADS, QK_DIM)).astype(BF16).T,
        "kv_a_norm": kv_a_norm.reshape(1, KV_LORA),
        "w_uk_t": pad_head(w_ukn).astype(BF16).T,
        "w_ukn_t": w_ukn.reshape(KV_LORA, N_HEADS * QK_NOPE).astype(BF16).T,
        "w_uv": w_uv,
        "w_uv_t": w_uv.T,
        "qgain_col": pad_gain(q_norm * (QK_DIM ** -0.5 * LOG2E)).reshape(HEAD_PAD, 1),
        "kgain_col": pad_gain(k_norm).reshape(HEAD_PAD, 1),
        "w_conv_out": w_conv_out.astype(BF16),
        "w_mla_out": w_mla_out.astype(BF16),
        "w_out_all": w_out_all.astype(BF16),
    }


def kernel(x_prompt, x_sample, cache_conv, cache_kv_latent, cache_k_rope, meta_tokens, ffn1_norm, ffn1_w_gate, ffn1_w_up, ffn1_w_down, mix_norm, w_in_all, conv_w, w_conv_out, q_a_norm, w_uq, kv_a_norm, w_ukv, q_norm, k_norm, w_mla_out, w_out_all, ffn2_norm, ffn2_w_gate, ffn2_w_up, ffn2_w_down):
    depth = ffn1_norm.shape[0]
    assert depth == 1
    nb, seq, _ = x_prompt.shape
    db, dseq, _ = x_sample.shape
    past = cache_kv_latent.shape[2]
    assert dseq == 2 * N_META and seq % PROJ_TM == 0 and seq % ATTN_TQ == 0
    assert (nb * seq) % FFN_TM == 0 and (nb * seq) % MERGE_TM == 0

    w = _prep_weights(mix_norm[0], w_in_all[0], conv_w[0], w_conv_out[0], q_a_norm[0], w_uq[0],
                      kv_a_norm[0], w_ukv[0], q_norm[0], k_norm[0], w_mla_out[0], w_out_all[0])
    ffn1 = (ffn1_norm[0].reshape(1, D_MODEL), ffn1_w_gate[0].astype(BF16), ffn1_w_up[0].astype(BF16),
            ffn1_w_down[0].astype(BF16))
    ffn2 = (ffn2_norm[0].reshape(1, D_MODEL), ffn2_w_gate[0].astype(BF16), ffn2_w_up[0].astype(BF16),
            ffn2_w_down[0].astype(BF16))

    n_s = db * dseq
    ns = db + 1
    xs = jnp.concatenate([x_sample.reshape(n_s, D_MODEL), jnp.zeros((dseq - N_META, D_MODEL), F32),
                          meta_tokens.astype(F32)], axis=0)
    xf = x_prompt.reshape(nb * seq, D_MODEL)

    x1s = _ffn(xs, *ffn1, tm=xs.shape[0])
    x1f = _ffn(xf, *ffn1, tm=FFN_TM, sub=FFN_SUB)

    pad_rows = lambda a: jnp.pad(a.reshape(ns, dseq, -1), ((0, 0), (0, SMALL_ROWS - dseq), (0, 0)))
    pos_s = N_META + past + jnp.arange(dseq, dtype=jnp.int32)
    pos_m = jnp.concatenate([jnp.zeros((dseq - N_META,), jnp.int32), jnp.arange(N_META, dtype=jnp.int32)])
    pos_small = jnp.pad(jnp.concatenate([jnp.tile(pos_s, db), pos_m]).reshape(ns, dseq),
                        ((0, 0), (0, SMALL_ROWS - dseq))).reshape(-1)
    rope_s = _rope_tables(pos_small)
    rope_f = _rope_tables(N_META + jnp.arange(seq, dtype=jnp.int32))

    cinit_s = jnp.concatenate([cache_conv[0].astype(F32), jnp.zeros((1, 2, D_CONV), F32)], axis=0)
    (bc_s, sgc_s, sgm_s, qt_s, k_s, vt_s, ckv_s, kpe_s, tail_s) = _proj(
        pad_rows(x1s).reshape(ns * SMALL_ROWS, D_MODEL), cinit_s, *rope_s, True, w, ns,
        SMALL_ROWS, SMALL_ROWS, SMALL_ROWS, dseq)
    cinit_f = jnp.broadcast_to(tail_s[db:db + 1], (nb, 2, D_CONV))
    (bc_f, sgc_f, sgm_f, qt_f, k_f, vt_f, ckv_f, kpe_f, tail_f) = _proj(
        x1f, cinit_f, *rope_f, False, w, nb, seq, PROJ_TM, PROJ_SUB, PROJ_SUB)

    unpad = lambda a: a.reshape(ns, SMALL_ROWS, -1)[:, :dseq]
    k_s3 = unpad(k_s)
    attn_f = _attn_prompt(qt_f, k_f.reshape(nb, seq, QK_W), vt_f, k_s3[db, dseq - N_META:],
                          vt_s[db, :, dseq - N_META:dseq], tq=ATTN_TQ, tk=ATTN_TK, nh=ATTN_HEADS)

    kpet_cache = cache_k_rope[0].astype(F32).transpose(0, 2, 1)
    q_s = qt_s[:db, :, :dseq].transpose(0, 2, 1).reshape(n_s, QK_W)
    v_s = vt_s[:db, :, :dseq].transpose(0, 2, 1).reshape(n_s, V_W)
    attn_s = _attn_sample(q_s, k_s3[:db].reshape(n_s, QK_W), v_s, cache_kv_latent[0].astype(F32),
                          kpet_cache, w, db, dseq)

    x2f = _merge(x1f, bc_f, attn_f.reshape(nb * seq, V_W), sgc_f, sgm_f, w, MERGE_TM, MERGE_SUB)
    sample_rows = lambda a: unpad(a)[:db].reshape(n_s, -1)
    x2s = _merge(x1s[:n_s], sample_rows(bc_s), attn_s, sample_rows(sgc_s), sample_rows(sgm_s), w, n_s, n_s)

    y_prompt = _ffn(x2f, *ffn2, tm=FFN_TM, sub=FFN_SUB).reshape(nb, seq, D_MODEL)
    y_sample = _ffn(x2s, *ffn2, tm=n_s).reshape(db, dseq, D_MODEL)

    ckv_s3, kpe_s3 = unpad(ckv_s), unpad(kpe_s)
    meta_rows = lambda a: jnp.broadcast_to(a[db, dseq - N_META:][None], (nb, N_META, a.shape[-1]))
    new_kv_p = jnp.concatenate([meta_rows(ckv_s3), ckv_f.reshape(nb, seq, KV_LORA)], axis=1)
    new_kpe_p = jnp.concatenate([meta_rows(kpe_s3), kpe_f.reshape(nb, seq, QK_ROPE)], axis=1)
    return (y_prompt, y_sample, tail_f[None], new_kv_p[None], new_kpe_p[None], tail_s[:db][None],
            ckv_s3[:db][None], kpe_s3[:db][None])
```
